```python
import math
import jax, jax.numpy as jnp
from jax import lax
import numpy as np

D_MODEL = 4096
BATCH = 2
SEQ = 8192
DEPTH = 2

N_A_LAYERS = DEPTH // 2
N_B_LAYERS = DEPTH - N_A_LAYERS
N_DENSE = (DEPTH + 1) // 2
N_MOE = DEPTH // 2

POOL_WINDOWS = (2, 4, 8, 16)
N_POOL_GROUPS = len(POOL_WINDOWS)
POOL_GROUP_DIM = D_MODEL // N_POOL_GROUPS

N_HEADS = D_MODEL // 256
HEAD_DIM = D_MODEL // N_HEADS // 2
V_DIM = 2 * HEAD_DIM
QK_WIDTH = N_HEADS * 2 * HEAD_DIM
V_WIDTH = N_HEADS * V_DIM
ROT_DIM = HEAD_DIM // 4
ROPE_THETA = 500000.0
Q_BLOCK = 128

D_FF = 256 * ((8 * D_MODEL // 3 + 255) // 256)
N_EXPERTS = 8
TOP_K = 2
D_FF_EXPERT = D_MODEL
MOE_BLOCK = 256

LN_EPS = 1e-5
DEEPNORM_ALPHA = (2.0 * DEPTH) ** 0.25
DEEPNORM_BETA = (8.0 * DEPTH) ** -0.25
ADA_SCALE = 0.1

kernel_name = "yoco_pool_diffattn_moe_block"


def layer_norm(x, g, b):
    xf = x.astype(jnp.float32)
    mu = jnp.mean(xf, axis=-1, keepdims=True)
    var = jnp.mean(jnp.square(xf - mu), axis=-1, keepdims=True)
    y = (xf - mu) * lax.rsqrt(var + LN_EPS) * g.astype(jnp.float32) + b.astype(jnp.float32)
    return y.astype(x.dtype)


def rms_norm(x, g):
    xf = x.astype(jnp.float32)
    y = xf * lax.rsqrt(jnp.mean(xf * xf, axis=-1, keepdims=True) + LN_EPS) * g.astype(jnp.float32)
    return y.astype(x.dtype)


def modulate(x, shift, scale):
    return x * (1 + scale[:, None, :]) + shift[:, None, :]


def rope_tables(positions):
    inv_freq = ROPE_THETA ** (-jnp.arange(0, ROT_DIM, 2, dtype=jnp.float32) / ROT_DIM)
    ang = positions.astype(jnp.float32)[..., None] * inv_freq
    return jnp.cos(ang), jnp.sin(ang)


def partial_rope(x, cos, sin):
    xr = x[..., :ROT_DIM].astype(jnp.float32)
    x1, x2 = jnp.split(xr, 2, axis=-1)
    c = cos[:, :, None, None, :]
    s = sin[:, :, None, None, :]
    rot = jnp.concatenate([x1 * c - x2 * s, x2 * c + x1 * s], axis=-1).astype(x.dtype)
    return jnp.concatenate([rot, x[..., ROT_DIM:]], axis=-1)


def lambda_init_fn(layer_idx):
    return 0.8 - 0.6 * math.exp(-0.3 * layer_idx)


def multiscale_pool_mixer(h, w_pool, pool_scale):
    b_, s_, d_ = h.shape
    hf = h.astype(jnp.float32).reshape(b_, s_, N_POOL_GROUPS, POOL_GROUP_DIM)
    csum = jnp.cumsum(hf, axis=1)
    t = jnp.arange(1, s_ + 1, dtype=jnp.float32)
    outs = []
    for g, w in enumerate(POOL_WINDOWS):
        cg = csum[:, :, g]
        lagged = jnp.pad(cg, ((0, 0), (w, 0), (0, 0)))[:, :s_]
        mean = (cg - lagged) / jnp.minimum(t, float(w))[None, :, None]
        outs.append(mean - hf[:, :, g])
    pooled = jnp.stack(outs, axis=2).astype(h.dtype)
    mixed = jnp.einsum('bsgc,gcd->bsgd', pooled, w_pool)
    return mixed.reshape(b_, s_, d_) * pool_scale


def shared_kv(x, cond, kv_ada_w, kv_ada_b, w_kv, cos, sin):
    b_, s_, _ = x.shape
    sh, sc = jnp.split(cond @ kv_ada_w + kv_ada_b, 2, axis=-1)
    kv = modulate(x, sh, sc) @ w_kv
    k = kv[..., :QK_WIDTH].reshape(b_, s_, N_HEADS, 2, HEAD_DIM)
    v = kv[..., QK_WIDTH:].reshape(b_, s_, N_HEADS, V_DIM)
    return partial_rope(k, cos, sin), v


def diff_attention(h, k, v, cos, sin, w_q, w_o, lam_q1, lam_k1, lam_q2, lam_k2, subln_g, lambda_init):
    b_, s_, _ = h.shape
    q = partial_rope((h @ w_q).reshape(b_, s_, N_HEADS, 2, HEAD_DIM), cos, sin)
    lam = (jnp.exp(jnp.sum(lam_q1.astype(jnp.float32) * lam_k1.astype(jnp.float32)))
           - jnp.exp(jnp.sum(lam_q2.astype(jnp.float32) * lam_k2.astype(jnp.float32)))
           + lambda_init)
    n_blk = s_ // Q_BLOCK
    q_blocks = q.reshape(b_, n_blk, Q_BLOCK, N_HEADS, 2, HEAD_DIM).transpose(1, 0, 2, 3, 4, 5)
    k_pos = jnp.arange(s_)
    scale = HEAD_DIM ** -0.5

    def block(args):
        qb, i = args
        sc = jnp.einsum('bqhcd,bkhcd->bhcqk', qb, k).astype(jnp.float32) * scale
        q_pos = i * Q_BLOCK + jnp.arange(Q_BLOCK)
        causal = k_pos[None, :] <= q_pos[:, None]
        p = jax.nn.softmax(jnp.where(causal, sc, -jnp.inf), axis=-1)
        p = p[:, :, 0] - lam * p[:, :, 1]
        return jnp.einsum('bhqk,bkhe->bqhe', p.astype(v.dtype), v)

    o = lax.map(block, (q_blocks, jnp.arange(n_blk)))
    o = o.transpose(1, 0, 2, 3, 4).reshape(b_, s_, N_HEADS, V_DIM)
    o = rms_norm(o, subln_g) * (1.0 - lambda_init)
    return o.reshape(b_, s_, V_WIDTH) @ w_o


def swiglu(h, w_gu, w_down):
    g, u = jnp.split(h @ w_gu, 2, axis=-1)
    return (jax.nn.silu(g) * u) @ w_down


def moe_ffn(h, w_router, w_gu_e, w_down_e):
    b_, s_, d_ = h.shape
    n_tok = b_ * s_
    xt = h.reshape(n_tok, d_)
    logits = (xt @ w_router).astype(jnp.float32)
    top_logit, top_idx = lax.top_k(logits, TOP_K)
    top_w = jax.nn.softmax(top_logit, axis=-1)
    n_asg = n_tok * TOP_K
    e_flat = top_idx.reshape(n_asg)
    tok_flat = jnp.arange(n_asg, dtype=jnp.int32) // TOP_K
    w_flat = top_w.reshape(n_asg)
    order = jnp.argsort(e_flat)
    e_sorted = e_flat[order]
    counts = jnp.zeros((N_EXPERTS,), jnp.int32).at[e_flat].add(1)
    starts = jnp.cumsum(counts) - counts
    padded = (counts + MOE_BLOCK - 1) // MOE_BLOCK * MOE_BLOCK
    padded_ends = jnp.cumsum(padded)
    padded_starts = padded_ends - padded
    dest = padded_starts[e_sorted] + (jnp.arange(n_asg, dtype=jnp.int32) - starts[e_sorted])
    n_rows = ((n_asg + MOE_BLOCK - 1) // MOE_BLOCK + N_EXPERTS) * MOE_BLOCK
    row_tok = jnp.full((n_rows,), n_tok, jnp.int32).at[dest].set(tok_flat[order])
    row_w = jnp.zeros((n_rows,), jnp.float32).at[dest].set(w_flat[order])
    n_blk = n_rows // MOE_BLOCK
    blk_start = jnp.arange(n_blk, dtype=jnp.int32) * MOE_BLOCK
    blk_expert = jnp.minimum(jnp.searchsorted(padded_ends, blk_start, side='right'), N_EXPERTS - 1)
    x_pad = jnp.concatenate([xt, jnp.zeros((1, d_), xt.dtype)], axis=0)
    x_rows = x_pad[row_tok].reshape(n_blk, MOE_BLOCK, d_)

    def expert_block(args):
        xb, e = args
        return swiglu(xb, w_gu_e[e], w_down_e[e])

    y_rows = lax.map(expert_block, (x_rows, blk_expert)).reshape(n_rows, d_)
    y_rows = y_rows * row_w[:, None].astype(y_rows.dtype)
    y = jnp.zeros((n_tok + 1, d_), y_rows.dtype).at[row_tok].add(y_rows)[:n_tok]
    return y.reshape(b_, s_, d_)


def setup_inputs(seed: int = 0) -> dict:
    key = jax.random.key(seed)
    ks = jax.random.split(key, 32)
    nrm = jax.random.normal
    D = D_MODEL
    x = nrm(ks[0], (BATCH, SEQ, D), jnp.float32)
    c = nrm(ks[1], (BATCH, D), jnp.float32)
    offset = jax.random.randint(ks[2], (BATCH, 1), 0, 1024, dtype=jnp.int32)
    positions = (jnp.arange(SEQ, dtype=jnp.int32)[None, :] + offset).astype(jnp.int32)
    ada_w = nrm(ks[3], (DEPTH, D, 6 * D), jnp.float32) * (D ** -0.5 * ADA_SCALE)
    ada_b = nrm(ks[4], (DEPTH, 6 * D), jnp.float32) * 0.01
    ln_g = 1.0 + 0.02 * nrm(ks[5], (DEPTH, 2, D), jnp.float32)
    ln_b = 0.02 * nrm(ks[6], (DEPTH, 2, D), jnp.float32)
    kv_ada_w = nrm(ks[7], (D, 2 * D), jnp.float32) * (D ** -0.5 * ADA_SCALE)
    kv_ada_b = nrm(ks[8], (2 * D,), jnp.float32) * 0.01
    w_pool = nrm(ks[9], (N_A_LAYERS, N_POOL_GROUPS, POOL_GROUP_DIM, POOL_GROUP_DIM), jnp.float32) * (POOL_GROUP_DIM ** -0.5 * DEEPNORM_BETA)
    pool_scale = 1.0 + 0.02 * nrm(ks[10], (N_A_LAYERS, D), jnp.float32)
    w_k = nrm(ks[11], (D, QK_WIDTH), jnp.float32) * D ** -0.5
    w_v = nrm(ks[12], (D, V_WIDTH), jnp.float32) * (D ** -0.5 * DEEPNORM_BETA)
    w_kv = jnp.concatenate([w_k, w_v], axis=1)
    w_q = nrm(ks[13], (N_B_LAYERS, D, QK_WIDTH), jnp.float32) * D ** -0.5
    w_o = nrm(ks[14], (N_B_LAYERS, V_WIDTH, D), jnp.float32) * (V_WIDTH ** -0.5 * DEEPNORM_BETA)
    lam_q1 = 0.1 * nrm(ks[15], (N_B_LAYERS, HEAD_DIM), jnp.float32)
    lam_k1 = 0.1 * nrm(ks[16], (N_B_LAYERS, HEAD_DIM), jnp.float32)
    lam_q2 = 0.1 * nrm(ks[17], (N_B_LAYERS, HEAD_DIM), jnp.float32)
    lam_k2 = 0.1 * nrm(ks[18], (N_B_LAYERS, HEAD_DIM), jnp.float32)
    subln_g = 1.0 + 0.02 * nrm(ks[19], (N_B_LAYERS, V_DIM), jnp.float32)
    ffn_w_gu = nrm(ks[20], (N_DENSE, D, 2 * D_FF), jnp.float32) * (D ** -0.5 * DEEPNORM_BETA)
    ffn_w_down = nrm(ks[21], (N_DENSE, D_FF, D), jnp.float32) * (D_FF ** -0.5 * DEEPNORM_BETA)
    router_w = nrm(ks[22], (N_MOE, D, N_EXPERTS), jnp.float32) * D ** -0.5
    moe_w_gu = nrm(ks[23], (N_MOE, N_EXPERTS, D, 2 * D_FF_EXPERT), jnp.float32) * (D ** -0.5 * DEEPNORM_BETA)
    moe_w_down = nrm(ks[24], (N_MOE, N_EXPERTS, D_FF_EXPERT, D), jnp.float32) * (D_FF_EXPERT ** -0.5 * DEEPNORM_BETA)
    return {"x": x, "c": c, "positions": positions, "ada_w": ada_w, "ada_b": ada_b,
            "ln_g": ln_g, "ln_b": ln_b, "kv_ada_w": kv_ada_w, "kv_ada_b": kv_ada_b,
            "w_pool": w_pool, "pool_scale": pool_scale, "w_kv": w_kv, "w_q": w_q, "w_o": w_o,
            "lam_q1": lam_q1, "lam_k1": lam_k1, "lam_q2": lam_q2, "lam_k2": lam_k2,
            "subln_g": subln_g, "ffn_w_gu": ffn_w_gu, "ffn_w_down": ffn_w_down,
            "router_w": router_w, "moe_w_gu": moe_w_gu, "moe_w_down": moe_w_down}


def reference(x, c, positions, ada_w, ada_b, ln_g, ln_b, kv_ada_w, kv_ada_b, w_pool, pool_scale,
              w_kv, w_q, w_o, lam_q1, lam_k1, lam_q2, lam_k2, subln_g, ffn_w_gu, ffn_w_down,
              router_w, moe_w_gu, moe_w_down):
    cos, sin = rope_tables(positions)
    cond = jax.nn.silu(c)
    k_sh = None
    v_sh = None
    for l in range(DEPTH):
        sh_m, sc_m, g_m, sh_f, sc_f, g_f = jnp.split(cond @ ada_w[l] + ada_b[l], 6, axis=-1)
        h = modulate(x, sh_m, sc_m)
        if l < N_A_LAYERS:
            y = multiscale_pool_mixer(h, w_pool[l], pool_scale[l])
        else:
            if l == N_A_LAYERS:
                k_sh, v_sh = shared_kv(x, cond, kv_ada_w, kv_ada_b, w_kv, cos, sin)
            j = l - N_A_LAYERS
            y = diff_attention(h, k_sh, v_sh, cos, sin, w_q[j], w_o[j], lam_q1[j], lam_k1[j],
                               lam_q2[j], lam_k2[j], subln_g[j], lambda_init_fn(l))
        x = layer_norm(DEEPNORM_ALPHA * x + (1 + g_m)[:, None, :] * y, ln_g[l, 0], ln_b[l, 0])
        h = modulate(x, sh_f, sc_f)
        if l % 2 == 0:
            y = swiglu(h, ffn_w_gu[l // 2], ffn_w_down[l // 2])
        else:
            y = moe_ffn(h, router_w[l // 2], moe_w_gu[l // 2], moe_w_down[l // 2])
        x = layer_norm(DEEPNORM_ALPHA * x + (1 + g_f)[:, None, :] * y, ln_g[l, 1], ln_b[l, 1])
    return x
```

```python
import functools
import math

import jax
import jax.numpy as jnp
from jax import lax
from jax.experimental import pallas as pl
from jax.experimental.pallas import tpu as pltpu

F32 = jnp.float32
BF16 = jnp.bfloat16

D_MODEL = 4096
BATCH = 2
SEQ = 8192
N_TOK = BATCH * SEQ
DEPTH = 2
POOL_WINDOWS = (2, 4, 8, 16)
POOL_GROUP_DIM = D_MODEL // len(POOL_WINDOWS)
POOL_HALO = 16
N_HEADS = 16
HEAD_DIM = 128
V_DIM = 2 * HEAD_DIM
ROT_DIM = HEAD_DIM // 4
ROT_HALF = ROT_DIM // 2
ROPE_THETA = 500000.0
D_FF = 11008
N_EXPERTS = 8
TOP_K = 2
D_FF_EXPERT = D_MODEL
LN_EPS = 1e-5
ALPHA = (2.0 * DEPTH) ** 0.25
ADA_CHUNKS = 6
LOG2E = math.log2(math.e)
Q_PRESCALE = HEAD_DIM ** -0.5 * LOG2E

V7X_VMEM_BYTES = 64 * 1024 * 1024
V7X_LANES = 128
V7X_SUBLANES = 8

ADA_TN = 1024
ADA_TK = 512
POOL_TS = 256
FFN_TM = 512
FFN_TF = 256
QKV_TM = 512
QKV_TN = 1024
ATT_TQ = 512
ATT_TK = 512
WO_TM = 512
WO_TK = 512
MOE_TM = 512
MOE_TF = 512
FIN_TM = 256
ACC_TN = 1024
EPI_ROWS = 64
ROUTER_PAD = V7X_LANES

MOE_N_ASG = N_TOK * TOP_K
MOE_N_BLK = MOE_N_ASG // MOE_TM + N_EXPERTS
MOE_N_ROWS = MOE_N_BLK * MOE_TM


def _vmem_limit(pipelined_bytes, resident_bytes=0, temp_bytes=0):
    need = 2 * pipelined_bytes + resident_bytes + temp_bytes + (2 << 20)
    assert need <= V7X_VMEM_BYTES - (2 << 20), need
    return int(need)


def _cparams(sem, vmem):
    return pltpu.CompilerParams(dimension_semantics=sem, vmem_limit_bytes=vmem)


def _layer_norm(z, g, b):
    mu = jnp.mean(z, axis=-1, keepdims=True)
    zc = z - mu
    var = jnp.mean(zc * zc, axis=-1, keepdims=True)
    return zc * lax.rsqrt(var + LN_EPS) * g + b


def _silu(x):
    return x / (1.0 + jnp.exp(-x))


def _for_row_chunks(n_rows, fn):
    def body(r, carry):
        fn(pl.ds(pl.multiple_of(r * EPI_ROWS, EPI_ROWS), EPI_ROWS))
        return carry
    lax.fori_loop(0, n_rows // EPI_ROWS, body, 0)


def _accumulate_dot(o_ref, a, w_ref):
    for j in range(o_ref.shape[1] // ACC_TN):
        cols = slice(j * ACC_TN, (j + 1) * ACC_TN)
        o_ref[:, cols] += jnp.dot(a, w_ref[:, cols], preferred_element_type=F32)


def _ada_kernel(c_ref, w_ref, b_ref, o_ref):
    acc = jnp.zeros(o_ref.shape, F32) + b_ref[...]
    for kk in range(c_ref.shape[1] // ADA_TK):
        ks = slice(kk * ADA_TK, (kk + 1) * ADA_TK)
        cond = _silu(c_ref[:, ks]).astype(BF16)
        acc = acc + jnp.dot(cond, w_ref[ks, :].astype(BF16), preferred_element_type=F32)
    o_ref[...] = acc


def _ada_call(c8, w, b):
    n_l, d, n = w.shape
    tn = ADA_TN
    blk = d * tn * 4 + 8 * d * 4 + 8 * tn * 4 + tn * 4
    return pl.pallas_call(
        _ada_kernel,
        out_shape=jax.ShapeDtypeStruct((n_l, 8, n), F32),
        grid=(n_l, n // tn),
        in_specs=[
            pl.BlockSpec((8, d), lambda l, j: (0, 0)),
            pl.BlockSpec((None, d, tn), lambda l, j: (l, 0, j)),
            pl.BlockSpec((None, 1, tn), lambda l, j: (l, 0, j)),
        ],
        out_specs=pl.BlockSpec((None, 8, tn), lambda l, j: (l, 0, j)),
        compiler_params=_cparams(("parallel", "parallel"),
                                 _vmem_limit(blk, temp_bytes=4 * ADA_TK * tn * 4)),
        name="ada_mod",
    )(c8, w, b.reshape(n_l, 1, n))


def _pool_kernel(x_ref, halo_ref, mod_ref, wp_ref, ps_ref, lng_ref, lnb_ref, o_ref):
    i = pl.program_id(1)
    ts = x_ref.shape[0]
    t1 = (i * ts + 1 + lax.broadcasted_iota(jnp.int32, (ts, 1), 0)).astype(F32)
    for g, w in enumerate(POOL_WINDOWS):
        cols = slice(g * POOL_GROUP_DIM, (g + 1) * POOL_GROUP_DIM)
        sh = mod_ref[0:1, cols]
        sc = mod_ref[1:2, cols]
        gate = mod_ref[2:3, cols]
        x = x_ref[:, cols]
        h = x * (1.0 + sc) + sh
        hh = jnp.where(i > 0, halo_ref[:, cols] * (1.0 + sc) + sh, 0.0)
        s = jnp.concatenate([hh, h], axis=0)
        span = 1
        while span < w:
            s = s + pltpu.roll(s, span, 0)
            span *= 2
        win = s[POOL_HALO:, :]
        pooled = win / jnp.minimum(t1, float(w)) - h
        mixed = jnp.dot(pooled.astype(BF16), wp_ref[g], preferred_element_type=F32)
        o_ref[:, cols] = ALPHA * x + (1.0 + gate) * (mixed * ps_ref[:, cols])

    def ln_rows(rows):
        o_ref[rows, :] = _layer_norm(o_ref[rows, :], lng_ref[...], lnb_ref[...])
    _for_row_chunks(ts, ln_rows)


def _pool_call(x, mod, wp, ps, lng, lnb):
    ts = POOL_TS
    n_s = SEQ // ts
    halo_per_tile = ts // POOL_HALO
    d = D_MODEL
    blk = ts * d * 4 * 2 + POOL_HALO * d * 4 + ADA_CHUNKS * d * 4 + 3 * d * 4
    return pl.pallas_call(
        _pool_kernel,
        out_shape=jax.ShapeDtypeStruct((N_TOK, d), F32),
        grid=(BATCH, n_s),
        in_specs=[
            pl.BlockSpec((ts, d), lambda b, i: (b * n_s + i, 0)),
            pl.BlockSpec((POOL_HALO, d),
                         lambda b, i: (jnp.maximum((b * n_s + i) * halo_per_tile - 1, 0), 0)),
            pl.BlockSpec((None, ADA_CHUNKS, d), lambda b, i: (b, 0, 0)),
            pl.BlockSpec(wp.shape, lambda b, i: (0, 0, 0), pipeline_mode=pl.Buffered(1)),
            pl.BlockSpec((1, d), lambda b, i: (0, 0)),
            pl.BlockSpec((1, d), lambda b, i: (0, 0)),
            pl.BlockSpec((1, d), lambda b, i: (0, 0)),
        ],
        out_specs=pl.BlockSpec((ts, d), lambda b, i: (b * n_s + i, 0)),
        compiler_params=_cparams(
            ("parallel", "parallel"),
            _vmem_limit(blk, wp.size * 2, 8 * (ts + POOL_HALO) * POOL_GROUP_DIM * 4)),
        name="pool_mixer",
    )(x, x, mod, wp, ps, lng, lnb)


def _ffn_kernel(x_ref, mod_ref, wg_ref, wu_ref, wd_ref, lng_ref, lnb_ref, o_ref, h_scr):
    k = pl.program_id(1)
    tm = x_ref.shape[0]

    @pl.when(k == 0)
    def _():
        def prep(rows):
            h_scr[rows, :] = (x_ref[rows, :] * (1.0 + mod_ref[4:5, :])
                              + mod_ref[3:4, :]).astype(BF16)
            o_ref[rows, :] = jnp.zeros((EPI_ROWS, o_ref.shape[1]), F32)
        _for_row_chunks(tm, prep)

    h = h_scr[...]
    g = jnp.dot(h, wg_ref[...], preferred_element_type=F32)
    u = jnp.dot(h, wu_ref[...], preferred_element_type=F32)
    _accumulate_dot(o_ref, (_silu(g) * u).astype(BF16), wd_ref)

    @pl.when(k == pl.num_programs(1) - 1)
    def _():
        def fin(rows):
            z = ALPHA * x_ref[rows, :] + (1.0 + mod_ref[5:6, :]) * o_ref[rows, :]
            o_ref[rows, :] = _layer_norm(z, lng_ref[...], lnb_ref[...])
        _for_row_chunks(tm, fin)


def _ffn_call(x, mod, w_gu, w_down, lng, lnb):
    tm, tf, d = FFN_TM, FFN_TF, D_MODEL
    n_f = D_FF // tf
    tiles_per_seq = SEQ // tm
    blk = tm * d * 4 * 2 + ADA_CHUNKS * d * 4 + 3 * d * tf * 2 + 2 * d * 4
    return pl.pallas_call(
        _ffn_kernel,
        out_shape=jax.ShapeDtypeStruct((N_TOK, d), F32),
        grid=(N_TOK // tm, n_f),
        in_specs=[
            pl.BlockSpec((tm, d), lambda i, k: (i, 0)),
            pl.BlockSpec((None, ADA_CHUNKS, d), lambda i, k: (i // tiles_per_seq, 0, 0)),
            pl.BlockSpec((d, tf), lambda i, k: (0, k)),
            pl.BlockSpec((d, tf), lambda i, k: (0, k + n_f)),
            pl.BlockSpec((tf, d), lambda i, k: (k, 0)),
            pl.BlockSpec((1, d), lambda i, k: (0, 0)),
            pl.BlockSpec((1, d), lambda i, k: (0, 0)),
        ],
        out_specs=pl.BlockSpec((tm, d), lambda i, k: (i, 0)),
        scratch_shapes=[pltpu.VMEM((tm, d), BF16)],
        compiler_params=_cparams(("parallel", "arbitrary"),
                                 _vmem_limit(blk, tm * d * 2, 6 * tm * tf * 4 + tm * ACC_TN * 4)),
        name="dense_ffn",
    )(x, mod, w_gu, w_gu, w_down, lng, lnb)


def _rope(r, c, s_up, s_dn):
    outs = []
    for j in range(r.shape[1] // HEAD_DIM):
        xc = r[:, j * HEAD_DIM:(j + 1) * HEAD_DIM]
        outs.append(xc * c + pltpu.roll(xc, ROT_HALF, 1) * s_up
                    + pltpu.roll(xc, HEAD_DIM - ROT_HALF, 1) * s_dn)
    return jnp.concatenate(outs, axis=1)


def _qkv_kernel(x_ref, modm_ref, modkv_ref, w_ref, c_ref, su_ref, sd_ref, o_ref,
                hq_scr, hkv_scr, *, n_q, n_k):
    n = pl.program_id(1)

    @pl.when(n == 0)
    def _():
        def prep(rows):
            x = x_ref[rows, :]
            hq_scr[rows, :] = (x * (1.0 + modm_ref[1:2, :]) + modm_ref[0:1, :]).astype(BF16)
            hkv_scr[rows, :] = (x * (1.0 + modkv_ref[1:2, :]) + modkv_ref[0:1, :]).astype(BF16)
        _for_row_chunks(x_ref.shape[0], prep)

    @pl.when(n < n_q)
    def _():
        r = jnp.dot(hq_scr[...], w_ref[...], preferred_element_type=F32)
        r = _rope(r, c_ref[...], su_ref[...], sd_ref[...]) * Q_PRESCALE
        o_ref[...] = r.astype(BF16)

    @pl.when((n >= n_q) & (n < n_q + n_k))
    def _():
        r = jnp.dot(hkv_scr[...], w_ref[...], preferred_element_type=F32)
        o_ref[...] = _rope(r, c_ref[...], su_ref[...], sd_ref[...]).astype(BF16)

    @pl.when(n >= n_q + n_k)
    def _():
        o_ref[...] = jnp.dot(hkv_scr[...], w_ref[...],
                             preferred_element_type=F32).astype(BF16)


def _qkv_call(x, modm, modkv, w_qkv, rope_c, rope_up, rope_dn):
    tm, tn, d = QKV_TM, QKV_TN, D_MODEL
    n_out = w_qkv.shape[1]
    tiles_per_seq = SEQ // tm
    blk = tm * d * 4 + (ADA_CHUNKS + 2) * d * 4 + d * tn * 2 + 3 * tm * HEAD_DIM * 4 + tm * tn * 2
    kern = functools.partial(_qkv_kernel, n_q=D_MODEL // tn, n_k=D_MODEL // tn)
    return pl.pallas_call(
        kern,
        out_shape=jax.ShapeDtypeStruct((N_TOK, n_out), BF16),
        grid=(N_TOK // tm, n_out // tn),
        in_specs=[
            pl.BlockSpec((tm, d), lambda i, n: (i, 0)),
            pl.BlockSpec((None, ADA_CHUNKS, d), lambda i, n: (i // tiles_per_seq, 0, 0)),
            pl.BlockSpec((None, 2, d), lambda i, n: (i // tiles_per_seq, 0, 0)),
            pl.BlockSpec((d, tn), lambda i, n: (0, n)),
            pl.BlockSpec((tm, HEAD_DIM), lambda i, n: (i, 0)),
            pl.BlockSpec((tm, HEAD_DIM), lambda i, n: (i, 0)),
            pl.BlockSpec((tm, HEAD_DIM), lambda i, n: (i, 0)),
        ],
        out_specs=pl.BlockSpec((tm, tn), lambda i, n: (i, n)),
        scratch_shapes=[pltpu.VMEM((tm, d), BF16), pltpu.VMEM((tm, d), BF16)],
        compiler_params=_cparams(("parallel", "arbitrary"),
                                 _vmem_limit(blk, 2 * tm * d * 2, 4 * tm * tn * 4)),
        name="qkv_proj",
    )(x, modm, modkv, w_qkv, rope_c, rope_up, rope_dn)


def _attn_kernel(q_ref, k_ref, v_ref, lam_ref, g_ref, o_ref, m_scr, l_scr, acc_scr,
                 *, lambda_init):
    i = pl.program_id(2)
    tq = q_ref.shape[0]
    tk = ATT_TK
    m_scr[...] = jnp.full(m_scr.shape, -jnp.inf, F32)
    l_scr[...] = jnp.zeros(l_scr.shape, F32)
    acc_scr[...] = jnp.zeros(acc_scr.shape, F32)

    def chunk(j, masked):
        start = pl.multiple_of(j * tk, tk)
        kc = k_ref[pl.ds(start, tk), :]
        vc = v_ref[pl.ds(start, tk), :]
        for c in range(2):
            qc = q_ref[:, c * HEAD_DIM:(c + 1) * HEAD_DIM]
            s = lax.dot_general(qc, kc[:, c * HEAD_DIM:(c + 1) * HEAD_DIM],
                                (((1,), (1,)), ((), ())), preferred_element_type=F32)
            if masked:
                row = lax.broadcasted_iota(jnp.int32, (tq, tk), 0)
                col = lax.broadcasted_iota(jnp.int32, (tq, tk), 1)
                s = jnp.where(col <= row, s, -jnp.inf)
            m_prev = m_scr[c]
            m_new = jnp.maximum(m_prev, jnp.max(s, axis=1, keepdims=True))
            a = jnp.exp2(m_prev - m_new)
            p = jnp.exp2(s - m_new)
            l_scr[c] = a * l_scr[c] + jnp.sum(p, axis=1, keepdims=True)
            acc_scr[c] = a * acc_scr[c] + jnp.dot(p.astype(BF16), vc,
                                                  preferred_element_type=F32)
            m_scr[c] = m_new

    def body(j, carry):
        chunk(j, False)
        return carry

    lax.fori_loop(0, i, body, 0)
    chunk(i, True)

    lam = (jnp.exp(jnp.sum(lam_ref[0:1, :] * lam_ref[1:2, :], axis=1, keepdims=True))
           - jnp.exp(jnp.sum(lam_ref[2:3, :] * lam_ref[3:4, :], axis=1, keepdims=True))
           + lambda_init)
    o = acc_scr[0] / l_scr[0] - lam * (acc_scr[1] / l_scr[1])
    ms = jnp.mean(o * o, axis=-1, keepdims=True)
    o_ref[...] = (o * lax.rsqrt(ms + LN_EPS) * g_ref[...] * (1.0 - lambda_init)).astype(BF16)


def _attn_call(qkv, lam_vecs, subln_g, lambda_init):
    assert ATT_TQ == ATT_TK
    tq = ATT_TQ
    n_q = SEQ // tq
    kern = functools.partial(_attn_kernel, lambda_init=lambda_init)
    blk = 2 * tq * V_DIM * 2 + 2 * SEQ * V_DIM * 2 + 4 * HEAD_DIM * 4 + V_DIM * 4
    scr = 2 * tq * V_DIM * 4 + 4 * tq * V7X_LANES * 4
    return pl.pallas_call(
        kern,
        out_shape=jax.ShapeDtypeStruct((N_TOK, D_MODEL), BF16),
        grid=(BATCH, N_HEADS, n_q),
        in_specs=[
            pl.BlockSpec((tq, V_DIM), lambda b, h, i: (b * n_q + i, h)),
            pl.BlockSpec((SEQ, V_DIM), lambda b, h, i: (b, N_HEADS + h)),
            pl.BlockSpec((SEQ, V_DIM), lambda b, h, i: (b, 2 * N_HEADS + h)),
            pl.BlockSpec((4, HEAD_DIM), lambda b, h, i: (0, 0)),
            pl.BlockSpec((1, V_DIM), lambda b, h, i: (0, 0)),
        ],
        out_specs=pl.BlockSpec((tq, V_DIM), lambda b, h, i: (b * n_q + i, h)),
        scratch_shapes=[pltpu.VMEM((2, tq, 1), F32), pltpu.VMEM((2, tq, 1), F32),
                        pltpu.VMEM((2, tq, V_DIM), F32)],
        compiler_params=_cparams(("parallel", "parallel", "arbitrary"),
                                 _vmem_limit(blk, scr, 8 * tq * ATT_TK * 4)),
        name="diff_attn",
    )(qkv, qkv, qkv, lam_vecs, subln_g)


def _wo_kernel(a_ref, w_ref, x_ref, mod_ref, lng_ref, lnb_ref, wr_ref,
               o_ref, h_ref, lg_ref):
    k = pl.program_id(1)
    tm = x_ref.shape[0]

    @pl.when(k == 0)
    def _():
        def prep(rows):
            o_ref[rows, :] = jnp.zeros((EPI_ROWS, o_ref.shape[1]), F32)
        _for_row_chunks(tm, prep)

    _accumulate_dot(o_ref, a_ref[...], w_ref)

    @pl.when(k == pl.num_programs(1) - 1)
    def _():
        def fin(rows):
            z = ALPHA * x_ref[rows, :] + (1.0 + mod_ref[2:3, :]) * o_ref[rows, :]
            xn = _layer_norm(z, lng_ref[...], lnb_ref[...])
            o_ref[rows, :] = xn
            h = xn * (1.0 + mod_ref[4:5, :]) + mod_ref[3:4, :]
            h_ref[rows, :] = h.astype(BF16)
            lg_ref[rows, :] = jnp.dot(h, wr_ref[...], precision=lax.Precision.HIGHEST,
                                      preferred_element_type=F32)
        _for_row_chunks(tm, fin)


def _wo_call(attn, w_o, x, mod, lng, lnb, w_router_pad):
    tm, tk, d = WO_TM, WO_TK, D_MODEL
    tiles_per_seq = SEQ // tm
    blk = (tm * tk * 2 + tk * d * 2 + tm * d * 4 * 2 + tm * d * 2 + ADA_CHUNKS * d * 4
           + 2 * d * 4 + tm * ROUTER_PAD * 4)
    return pl.pallas_call(
        _wo_kernel,
        out_shape=(jax.ShapeDtypeStruct((N_TOK, d), F32),
                   jax.ShapeDtypeStruct((N_TOK, d), BF16),
                   jax.ShapeDtypeStruct((N_TOK, ROUTER_PAD), F32)),
        grid=(N_TOK // tm, d // tk),
        in_specs=[
            pl.BlockSpec((tm, tk), lambda i, k: (i, k)),
            pl.BlockSpec((tk, d), lambda i, k: (k, 0)),
            pl.BlockSpec((tm, d), lambda i, k: (i, 0)),
            pl.BlockSpec((None, ADA_CHUNKS, d), lambda i, k: (i // tiles_per_seq, 0, 0)),
            pl.BlockSpec((1, d), lambda i, k: (0, 0)),
            pl.BlockSpec((1, d), lambda i, k: (0, 0)),
            pl.BlockSpec((d, ROUTER_PAD), lambda i, k: (0, 0), pipeline_mode=pl.Buffered(1)),
        ],
        out_specs=(pl.BlockSpec((tm, d), lambda i, k: (i, 0)),
                   pl.BlockSpec((tm, d), lambda i, k: (i, 0)),
                   pl.BlockSpec((tm, ROUTER_PAD), lambda i, k: (i, 0))),
        compiler_params=_cparams(("parallel", "arbitrary"),
                                 _vmem_limit(blk, d * ROUTER_PAD * 4,
                                             tm * ACC_TN * 4 + 6 * EPI_ROWS * d * 4)),
        name="attn_out_proj",
    )(attn, w_o, x, mod, lng, lnb, w_router_pad)


def _moe_kernel(be_ref, nu_ref, h_ref, rw_ref, wg_ref, wu_ref, wd_ref, o_ref):
    i = pl.program_id(0)
    k = pl.program_id(1)
    tm = h_ref.shape[0]

    @pl.when(i < nu_ref[0])
    def _():
        @pl.when(k == 0)
        def _():
            def prep(rows):
                o_ref[rows, :] = jnp.zeros((EPI_ROWS, o_ref.shape[1]), F32)
            _for_row_chunks(tm, prep)

        h = h_ref[...]
        g = jnp.dot(h, wg_ref[...], preferred_element_type=F32)
        u = jnp.dot(h, wu_ref[...], preferred_element_type=F32)
        _accumulate_dot(o_ref, (_silu(g) * u).astype(BF16), wd_ref)

        @pl.when(k == pl.num_programs(1) - 1)
        def _():
            def fin(rows):
                o_ref[rows, :] = o_ref[rows, :] * rw_ref[rows, :]
            _for_row_chunks(tm, fin)


def _moe_call(blk_expert, n_used, h_rows, row_w, w_gu, w_down):
    tm, tf, d = MOE_TM, MOE_TF, D_MODEL
    n_f = D_FF_EXPERT // tf

    def row_blk(i, nu):
        return jnp.minimum(i, nu[0] - 1)

    def f_blk(i, k, nu):
        return jnp.where(i < nu[0], k, n_f - 1)

    blk = tm * d * 2 + tm * V7X_LANES * 4 + 3 * d * tf * 2 + tm * d * 4
    grid_spec = pltpu.PrefetchScalarGridSpec(
        num_scalar_prefetch=2,
        grid=(MOE_N_BLK, n_f),
        in_specs=[
            pl.BlockSpec((tm, d), lambda i, k, be, nu: (row_blk(i, nu), 0)),
            pl.BlockSpec((tm, 1), lambda i, k, be, nu: (row_blk(i, nu), 0)),
            pl.BlockSpec((None, d, tf),
                         lambda i, k, be, nu: (be[row_blk(i, nu)], 0, f_blk(i, k, nu))),
            pl.BlockSpec((None, d, tf),
                         lambda i, k, be, nu: (be[row_blk(i, nu)], 0, f_blk(i, k, nu) + n_f)),
            pl.BlockSpec((None, tf, d),
                         lambda i, k, be, nu: (be[row_blk(i, nu)], f_blk(i, k, nu), 0)),
        ],
        out_specs=pl.BlockSpec((tm, d), lambda i, k, be, nu: (row_blk(i, nu), 0)),
    )
    return pl.pallas_call(
        _moe_kernel,
        out_shape=jax.ShapeDtypeStruct((MOE_N_ROWS, d), F32),
        grid_spec=grid_spec,
        compiler_params=_cparams(("arbitrary", "arbitrary"),
                                 _vmem_limit(blk, 0, 6 * tm * tf * 4 + tm * ACC_TN * 4)),
        name="moe_ffn",
    )(blk_expert, n_used, h_rows, row_w, w_gu, w_gu, w_down)


def _final_kernel(x_ref, y_ref, mod_ref, lng_ref, lnb_ref, o_ref):
    def fin(rows):
        z = ALPHA * x_ref[rows, :] + (1.0 + mod_ref[5:6, :]) * y_ref[rows, :]
        o_ref[rows, :] = _layer_norm(z, lng_ref[...], lnb_ref[...])
    _for_row_chunks(x_ref.shape[0], fin)


def _final_call(x, y, mod, lng, lnb):
    tm, d = FIN_TM, D_MODEL
    tiles_per_seq = SEQ // tm
    blk = 3 * tm * d * 4 + ADA_CHUNKS * d * 4 + 2 * d * 4
    return pl.pallas_call(
        _final_kernel,
        out_shape=jax.ShapeDtypeStruct((N_TOK, d), F32),
        grid=(N_TOK // tm,),
        in_specs=[
            pl.BlockSpec((tm, d), lambda i: (i, 0)),
            pl.BlockSpec((tm, d), lambda i: (i, 0)),
            pl.BlockSpec((None, ADA_CHUNKS, d), lambda i: (i // tiles_per_seq, 0, 0)),
            pl.BlockSpec((1, d), lambda i: (0, 0)),
            pl.BlockSpec((1, d), lambda i: (0, 0)),
        ],
        out_specs=pl.BlockSpec((tm, d), lambda i: (i, 0)),
        compiler_params=_cparams(("parallel",), _vmem_limit(blk, 0, 8 * EPI_ROWS * d * 4)),
        name="final_ln",
    )(x, y, mod, lng, lnb)


def _route(logits):
    top_logit, top_idx = lax.top_k(logits, TOP_K)
    top_w = jax.nn.softmax(top_logit, axis=-1)
    e_flat = top_idx.reshape(MOE_N_ASG).astype(jnp.int32)
    w_flat = top_w.reshape(MOE_N_ASG)
    tok_flat = jnp.arange(MOE_N_ASG, dtype=jnp.int32) // TOP_K
    onehot = (e_flat[:, None] == jnp.arange(N_EXPERTS, dtype=jnp.int32)[None, :]).astype(jnp.int32)
    csum = jnp.cumsum(onehot, axis=0)
    rank = jnp.sum((csum - onehot) * onehot, axis=1)
    counts = csum[-1]
    padded = (counts + MOE_TM - 1) // MOE_TM * MOE_TM
    padded_ends = jnp.cumsum(padded)
    padded_starts = padded_ends - padded
    dest = padded_starts[e_flat] + rank
    n_used = (padded_ends[-1:] // MOE_TM).astype(jnp.int32)
    blk_start = jnp.arange(MOE_N_BLK, dtype=jnp.int32) * MOE_TM
    blk_expert = jnp.minimum(jnp.searchsorted(padded_ends, blk_start, side='right'),
                             N_EXPERTS - 1).astype(jnp.int32)
    row_tok = jnp.zeros((MOE_N_ROWS,), jnp.int32).at[dest].set(tok_flat)
    row_w = jnp.zeros((MOE_N_ROWS,), F32).at[dest].set(w_flat)
    return blk_expert, n_used, dest, row_tok, row_w


def _rope_tables(positions):
    inv_freq = ROPE_THETA ** (-jnp.arange(0, ROT_DIM, 2, dtype=F32) / ROT_DIM)
    ang = positions.astype(F32)[..., None] * inv_freq
    cos = jnp.cos(ang).reshape(N_TOK, ROT_HALF)
    sin = jnp.sin(ang).reshape(N_TOK, ROT_HALF)
    rest = HEAD_DIM - ROT_DIM
    c = jnp.concatenate([cos, cos, jnp.ones((N_TOK, rest), F32)], axis=1)
    s_up = jnp.concatenate([jnp.zeros((N_TOK, ROT_HALF), F32), sin,
                            jnp.zeros((N_TOK, rest), F32)], axis=1)
    s_dn = jnp.concatenate([-sin, jnp.zeros((N_TOK, HEAD_DIM - ROT_HALF), F32)], axis=1)
    return c, s_up, s_dn


def kernel(x, c, positions, ada_w, ada_b, ln_g, ln_b, kv_ada_w, kv_ada_b, w_pool, pool_scale,
           w_kv, w_q, w_o, lam_q1, lam_k1, lam_q2, lam_k2, subln_g, ffn_w_gu, ffn_w_down,
           router_w, moe_w_gu, moe_w_down):
    d = D_MODEL
    xt = x.reshape(N_TOK, d)

    c8 = jnp.pad(c, ((0, V7X_SUBLANES - BATCH), (0, 0)))
    mods = _ada_call(c8, ada_w, ada_b)[:, :BATCH].reshape(DEPTH, BATCH, ADA_CHUNKS, d)
    mod_kv = _ada_call(c8, kv_ada_w[None], kv_ada_b[None])[0, :BATCH].reshape(BATCH, 2, d)

    x1 = _pool_call(xt, mods[0], w_pool[0].astype(BF16), pool_scale[0][None],
                    ln_g[0, 0][None], ln_b[0, 0][None])
    x2 = _ffn_call(x1, mods[0], ffn_w_gu[0].astype(BF16), ffn_w_down[0].astype(BF16),
                   ln_g[0, 1][None], ln_b[0, 1][None])

    rope_c, rope_up, rope_dn = _rope_tables(positions)
    w_qkv = jnp.concatenate([w_q[0].astype(BF16), w_kv.astype(BF16)], axis=1)
    qkv = _qkv_call(x2, mods[1], mod_kv, w_qkv, rope_c, rope_up, rope_dn)
    lam_vecs = jnp.stack([lam_q1[0], lam_k1[0], lam_q2[0], lam_k2[0]], axis=0)
    lambda_init = 0.8 - 0.6 * math.exp(-0.3 * 1)
    attn = _attn_call(qkv, lam_vecs, subln_g[0][None], lambda_init)
    w_router_pad = jnp.pad(router_w[0], ((0, 0), (0, ROUTER_PAD - N_EXPERTS)))
    x3, h3, logits = _wo_call(attn, w_o[0].astype(BF16), x2, mods[1],
                              ln_g[1, 0][None], ln_b[1, 0][None], w_router_pad)

    blk_expert, n_used, dest, row_tok, row_w = _route(logits[:, :N_EXPERTS])
    h_rows = jnp.take(h3, row_tok, axis=0)
    y_rows = _moe_call(blk_expert, n_used, h_rows, row_w[:, None],
                       moe_w_gu[0].astype(BF16), moe_w_down[0].astype(BF16))
    y = jnp.take(y_rows, dest, axis=0).reshape(N_TOK, TOP_K, d).sum(axis=1)
    out = _final_call(x3, y, mods[1], ln_g[1, 1][None], ln_b[1, 1][None])
    return out.reshape(BATCH, SEQ, d)
```

```python
import functools
import math

import jax
import jax.numpy as jnp
from jax import lax
from jax.experimental import pallas as pl
from jax.experimental.pallas import tpu as pltpu

F32 = jnp.float32
BF16 = jnp.bfloat16

D_MODEL = 4096
BATCH = 2
SEQ = 8192
N_TOK = BATCH * SEQ
DEPTH = 2
POOL_WINDOWS = (2, 4, 8, 16)
POOL_GROUP_DIM = D_MODEL // len(POOL_WINDOWS)
POOL_HALO = 16
N_HEADS = 16
HEAD_DIM = 128
V_DIM = 2 * HEAD_DIM
ROT_DIM = HEAD_DIM // 4
ROT_HALF = ROT_DIM // 2
ROPE_THETA = 500000.0
D_FF = 11008
N_EXPERTS = 8
TOP_K = 2
D_FF_EXPERT = D_MODEL
LN_EPS = 1e-5
ALPHA = (2.0 * DEPTH) ** 0.25
ADA_CHUNKS = 6
LOG2E = math.log2(math.e)
Q_PRESCALE = HEAD_DIM ** -0.5 * LOG2E

V7X_VMEM_BYTES = 64 * 1024 * 1024
V7X_LANES = 128
V7X_SUBLANES = 8

ADA_TN = 1024
ADA_TK = 512
POOL_TS = 256
FFN_TM = 512
FFN_TF = 256
QKV_TM = 512
QKV_TN = 1024
ATT_TQ = 512
ATT_TK = 1024
WO_TM = 512
WO_TK = 512
MOE_TM = 512
MOE_TF = 512
FIN_TM = 256
ACC_TN = 1024
EPI_ROWS = 64
ROUTER_PAD = V7X_LANES

MOE_N_ASG = N_TOK * TOP_K
MOE_N_BLK = MOE_N_ASG // MOE_TM + N_EXPERTS
MOE_N_ROWS = MOE_N_BLK * MOE_TM


def _vmem_limit(pipelined_bytes, resident_bytes=0, temp_bytes=0):
    need = 2 * pipelined_bytes + resident_bytes + temp_bytes + (2 << 20)
    assert need <= V7X_VMEM_BYTES - (2 << 20), need
    return int(need)


def _cparams(sem, vmem):
    return pltpu.CompilerParams(dimension_semantics=sem, vmem_limit_bytes=vmem)


def _layer_norm(z, g, b):
    mu = jnp.mean(z, axis=-1, keepdims=True)
    zc = z - mu
    var = jnp.mean(zc * zc, axis=-1, keepdims=True)
    return zc * lax.rsqrt(var + LN_EPS) * g + b


def _silu(x):
    return x / (1.0 + jnp.exp(-x))


def _for_row_chunks(n_rows, fn):
    def body(r, carry):
        fn(pl.ds(pl.multiple_of(r * EPI_ROWS, EPI_ROWS), EPI_ROWS))
        return carry
    lax.fori_loop(0, n_rows // EPI_ROWS, body, 0)


def _accumulate_dot(o_ref, a, w_ref):
    for j in range(o_ref.shape[1] // ACC_TN):
        cols = slice(j * ACC_TN, (j + 1) * ACC_TN)
        o_ref[:, cols] += jnp.dot(a, w_ref[:, cols], preferred_element_type=F32)


def _ada_kernel(c_ref, w_ref, b_ref, o_ref):
    acc = jnp.zeros(o_ref.shape, F32) + b_ref[...]
    for kk in range(c_ref.shape[1] // ADA_TK):
        ks = slice(kk * ADA_TK, (kk + 1) * ADA_TK)
        cond = _silu(c_ref[:, ks]).astype(BF16)
        acc = acc + jnp.dot(cond, w_ref[ks, :].astype(BF16), preferred_element_type=F32)
    o_ref[...] = acc


def _ada_call(c8, w, b):
    n_l, d, n = w.shape
    tn = ADA_TN
    blk = d * tn * 4 + 8 * d * 4 + 8 * tn * 4 + tn * 4
    return pl.pallas_call(
        _ada_kernel,
        out_shape=jax.ShapeDtypeStruct((n_l, 8, n), F32),
        grid=(n_l, n // tn),
        in_specs=[
            pl.BlockSpec((8, d), lambda l, j: (0, 0)),
            pl.BlockSpec((None, d, tn), lambda l, j: (l, 0, j)),
            pl.BlockSpec((None, 1, tn), lambda l, j: (l, 0, j)),
        ],
        out_specs=pl.BlockSpec((None, 8, tn), lambda l, j: (l, 0, j)),
        compiler_params=_cparams(("parallel", "parallel"),
                                 _vmem_limit(blk, temp_bytes=4 * ADA_TK * tn * 4)),
        name="ada_mod",
    )(c8, w, b.reshape(n_l, 1, n))


def _pool_kernel(x_ref, halo_ref, mod_ref, wp_ref, ps_ref, lng_ref, lnb_ref, o_ref):
    i = pl.program_id(1)
    ts = x_ref.shape[0]
    t1 = (i * ts + 1 + lax.broadcasted_iota(jnp.int32, (ts, 1), 0)).astype(F32)
    for g, w in enumerate(POOL_WINDOWS):
        cols = slice(g * POOL_GROUP_DIM, (g + 1) * POOL_GROUP_DIM)
        sh = mod_ref[0:1, cols]
        sc = mod_ref[1:2, cols]
        gate = mod_ref[2:3, cols]
        x = x_ref[:, cols]
        h = x * (1.0 + sc) + sh
        hh = jnp.where(i > 0, halo_ref[:, cols] * (1.0 + sc) + sh, 0.0)
        s = jnp.concatenate([hh, h], axis=0)
        span = 1
        while span < w:
            s = s + pltpu.roll(s, span, 0)
            span *= 2
        win = s[POOL_HALO:, :]
        pooled = win / jnp.minimum(t1, float(w)) - h
        mixed = jnp.dot(pooled.astype(BF16), wp_ref[g], preferred_element_type=F32)
        o_ref[:, cols] = ALPHA * x + (1.0 + gate) * (mixed * ps_ref[:, cols])

    def ln_rows(rows):
        o_ref[rows, :] = _layer_norm(o_ref[rows, :], lng_ref[...], lnb_ref[...])
    _for_row_chunks(ts, ln_rows)


def _pool_call(x, mod, wp, ps, lng, lnb):
    ts = POOL_TS
    n_s = SEQ // ts
    halo_per_tile = ts // POOL_HALO
    d = D_MODEL
    blk = ts * d * 4 * 2 + POOL_HALO * d * 4 + ADA_CHUNKS * d * 4 + 3 * d * 4
    return pl.pallas_call(
        _pool_kernel,
        out_shape=jax.ShapeDtypeStruct((N_TOK, d), F32),
        grid=(BATCH, n_s),
        in_specs=[
            pl.BlockSpec((ts, d), lambda b, i: (b * n_s + i, 0)),
            pl.BlockSpec((POOL_HALO, d),
                         lambda b, i: (jnp.maximum((b * n_s + i) * halo_per_tile - 1, 0), 0)),
            pl.BlockSpec((None, ADA_CHUNKS, d), lambda b, i: (b, 0, 0)),
            pl.BlockSpec(wp.shape, lambda b, i: (0, 0, 0), pipeline_mode=pl.Buffered(1)),
            pl.BlockSpec((1, d), lambda b, i: (0, 0)),
            pl.BlockSpec((1, d), lambda b, i: (0, 0)),
            pl.BlockSpec((1, d), lambda b, i: (0, 0)),
        ],
        out_specs=pl.BlockSpec((ts, d), lambda b, i: (b * n_s + i, 0)),
        compiler_params=_cparams(
            ("parallel", "parallel"),
            _vmem_limit(blk, wp.size * 2, 8 * (ts + POOL_HALO) * POOL_GROUP_DIM * 4)),
        name="pool_mixer",
    )(x, x, mod, wp, ps, lng, lnb)


def _ffn_kernel(x_ref, mod_ref, wg_ref, wu_ref, wd_ref, lng_ref, lnb_ref, o_ref, h_scr):
    k = pl.program_id(1)
    tm = x_ref.shape[0]

    @pl.when(k == 0)
    def _():
        def prep(rows):
            h_scr[rows, :] = (x_ref[rows, :] * (1.0 + mod_ref[4:5, :])
                              + mod_ref[3:4, :]).astype(BF16)
            o_ref[rows, :] = jnp.zeros((EPI_ROWS, o_ref.shape[1]), F32)
        _for_row_chunks(tm, prep)

    h = h_scr[...]
    g = jnp.dot(h, wg_ref[...], preferred_element_type=F32)
    u = jnp.dot(h, wu_ref[...], preferred_element_type=F32)
    _accumulate_dot(o_ref, (_silu(g) * u).astype(BF16), wd_ref)

    @pl.when(k == pl.num_programs(1) - 1)
    def _():
        def fin(rows):
            z = ALPHA * x_ref[rows, :] + (1.0 + mod_ref[5:6, :]) * o_ref[rows, :]
            o_ref[rows, :] = _layer_norm(z, lng_ref[...], lnb_ref[...])
        _for_row_chunks(tm, fin)


def _ffn_call(x, mod, w_gu, w_down, lng, lnb):
    tm, tf, d = FFN_TM, FFN_TF, D_MODEL
    n_f = D_FF // tf
    tiles_per_seq = SEQ // tm
    blk = tm * d * 4 * 2 + ADA_CHUNKS * d * 4 + 3 * d * tf * 2 + 2 * d * 4
    return pl.pallas_call(
        _ffn_kernel,
        out_shape=jax.ShapeDtypeStruct((N_TOK, d), F32),
        grid=(N_TOK // tm, n_f),
        in_specs=[
            pl.BlockSpec((tm, d), lambda i, k: (i, 0)),
            pl.BlockSpec((None, ADA_CHUNKS, d), lambda i, k: (i // tiles_per_seq, 0, 0)),
            pl.BlockSpec((d, tf), lambda i, k: (0, k)),
            pl.BlockSpec((d, tf), lambda i, k: (0, k + n_f)),
            pl.BlockSpec((tf, d), lambda i, k: (k, 0)),
            pl.BlockSpec((1, d), lambda i, k: (0, 0)),
            pl.BlockSpec((1, d), lambda i, k: (0, 0)),
        ],
        out_specs=pl.BlockSpec((tm, d), lambda i, k: (i, 0)),
        scratch_shapes=[pltpu.VMEM((tm, d), BF16)],
        compiler_params=_cparams(("parallel", "arbitrary"),
                                 _vmem_limit(blk, tm * d * 2, 6 * tm * tf * 4 + tm * ACC_TN * 4)),
        name="dense_ffn",
    )(x, mod, w_gu, w_gu, w_down, lng, lnb)


def _rope(r, c, s_up, s_dn):
    outs = []
    for j in range(r.shape[1] // HEAD_DIM):
        xc = r[:, j * HEAD_DIM:(j + 1) * HEAD_DIM]
        outs.append(xc * c + pltpu.roll(xc, ROT_HALF, 1) * s_up
                    + pltpu.roll(xc, HEAD_DIM - ROT_HALF, 1) * s_dn)
    return jnp.concatenate(outs, axis=1)


def _qkv_kernel(x_ref, modm_ref, modkv_ref, w_ref, c_ref, su_ref, sd_ref, o_ref,
                hq_scr, hkv_scr, *, n_q, n_k):
    n = pl.program_id(1)

    @pl.when(n == 0)
    def _():
        def prep(rows):
            x = x_ref[rows, :]
            hq_scr[rows, :] = (x * (1.0 + modm_ref[1:2, :]) + modm_ref[0:1, :]).astype(BF16)
            hkv_scr[rows, :] = (x * (1.0 + modkv_ref[1:2, :]) + modkv_ref[0:1, :]).astype(BF16)
        _for_row_chunks(x_ref.shape[0], prep)

    @pl.when(n < n_q)
    def _():
        r = jnp.dot(hq_scr[...], w_ref[...], preferred_element_type=F32)
        r = _rope(r, c_ref[...], su_ref[...], sd_ref[...]) * Q_PRESCALE
        o_ref[...] = r.astype(BF16)

    @pl.when((n >= n_q) & (n < n_q + n_k))
    def _():
        r = jnp.dot(hkv_scr[...], w_ref[...], preferred_element_type=F32)
        o_ref[...] = _rope(r, c_ref[...], su_ref[...], sd_ref[...]).astype(BF16)

    @pl.when(n >= n_q + n_k)
    def _():
        o_ref[...] = jnp.dot(hkv_scr[...], w_ref[...],
                             preferred_element_type=F32).astype(BF16)


def _qkv_call(x, modm, modkv, w_qkv, rope_c, rope_up, rope_dn):
    tm, tn, d = QKV_TM, QKV_TN, D_MODEL
    n_out = w_qkv.shape[1]
    tiles_per_seq = SEQ // tm
    blk = tm * d * 4 + (ADA_CHUNKS + 2) * d * 4 + d * tn * 2 + 3 * tm * HEAD_DIM * 4 + tm * tn * 2
    kern = functools.partial(_qkv_kernel, n_q=D_MODEL // tn, n_k=D_MODEL // tn)
    return pl.pallas_call(
        kern,
        out_shape=jax.ShapeDtypeStruct((N_TOK, n_out), BF16),
        grid=(N_TOK // tm, n_out // tn),
        in_specs=[
            pl.BlockSpec((tm, d), lambda i, n: (i, 0)),
            pl.BlockSpec((None, ADA_CHUNKS, d), lambda i, n: (i // tiles_per_seq, 0, 0)),
            pl.BlockSpec((None, 2, d), lambda i, n: (i // tiles_per_seq, 0, 0)),
            pl.BlockSpec((d, tn), lambda i, n: (0, n)),
            pl.BlockSpec((tm, HEAD_DIM), lambda i, n: (i, 0)),
            pl.BlockSpec((tm, HEAD_DIM), lambda i, n: (i, 0)),
            pl.BlockSpec((tm, HEAD_DIM), lambda i, n: (i, 0)),
        ],
        out_specs=pl.BlockSpec((tm, tn), lambda i, n: (i, n)),
        scratch_shapes=[pltpu.VMEM((tm, d), BF16), pltpu.VMEM((tm, d), BF16)],
        compiler_params=_cparams(("parallel", "arbitrary"),
                                 _vmem_limit(blk, 2 * tm * d * 2, 4 * tm * tn * 4)),
        name="qkv_proj",
    )(x, modm, modkv, w_qkv, rope_c, rope_up, rope_dn)


def _attn_kernel(q_ref, k_ref, v_ref, lam_ref, g_ref, o_ref, m_scr, l_scr, acc_scr,
                 *, lambda_init):
    i = pl.program_id(2)
    tq = q_ref.shape[0]
    tk = ATT_TK
    m_scr[...] = jnp.full(m_scr.shape, -jnp.inf, F32)
    l_scr[...] = jnp.zeros(l_scr.shape, F32)
    acc_scr[...] = jnp.zeros(acc_scr.shape, F32)

    def chunk(start, width, masked):
        kc = k_ref[pl.ds(start, width), :]
        vc = v_ref[pl.ds(start, width), :]
        for c in range(2):
            qc = q_ref[:, c * HEAD_DIM:(c + 1) * HEAD_DIM]
            s = lax.dot_general(qc, kc[:, c * HEAD_DIM:(c + 1) * HEAD_DIM],
                                (((1,), (1,)), ((), ())), preferred_element_type=F32)
            if masked:
                row = lax.broadcasted_iota(jnp.int32, (tq, width), 0)
                col = lax.broadcasted_iota(jnp.int32, (tq, width), 1)
                s = jnp.where(col <= row, s, -jnp.inf)
            lanes = [s[:, t * V7X_LANES:(t + 1) * V7X_LANES] for t in range(width // V7X_LANES)]
            m_prev = m_scr[c]
            m_new = jnp.maximum(m_prev, jnp.max(functools.reduce(jnp.maximum, lanes),
                                                axis=1, keepdims=True))
            a = jnp.exp2(m_prev - m_new)
            ps = [jnp.exp2(x - m_new) for x in lanes]
            l_scr[c] = a * l_scr[c] + functools.reduce(jnp.add, ps)
            p = jnp.concatenate([x.astype(BF16) for x in ps], axis=1)
            acc_scr[c] = (jnp.concatenate([a, a], axis=1) * acc_scr[c]
                          + jnp.dot(p, vc, preferred_element_type=F32))
            m_scr[c] = m_new

    per_wide = tk // tq
    n_wide = i // per_wide

    def body(j, carry):
        chunk(pl.multiple_of(2 * j * tk, tk), tk, False)
        chunk(pl.multiple_of((2 * j + 1) * tk, tk), tk, False)
        return carry

    lax.fori_loop(0, n_wide // 2, body, 0)

    @pl.when(n_wide % 2 == 1)
    def _():
        chunk(pl.multiple_of((n_wide - 1) * tk, tk), tk, False)

    for r in range(per_wide - 1):
        @pl.when(i - n_wide * per_wide > r)
        def _():
            chunk(pl.multiple_of((n_wide * per_wide + r) * tq, tq), tq, False)
    chunk(pl.multiple_of(i * tq, tq), tq, True)

    lam = (jnp.exp(jnp.sum(lam_ref[0:1, :] * lam_ref[1:2, :], axis=1, keepdims=True))
           - jnp.exp(jnp.sum(lam_ref[2:3, :] * lam_ref[3:4, :], axis=1, keepdims=True))
           + lambda_init)
    l0 = jnp.sum(l_scr[0], axis=1, keepdims=True)
    l1 = jnp.sum(l_scr[1], axis=1, keepdims=True)
    o = acc_scr[0] / l0 - lam * (acc_scr[1] / l1)
    ms = jnp.mean(o * o, axis=-1, keepdims=True)
    o_ref[...] = (o * lax.rsqrt(ms + LN_EPS) * g_ref[...] * (1.0 - lambda_init)).astype(BF16)


def _attn_call(qkv, lam_vecs, subln_g, lambda_init):
    assert ATT_TK % ATT_TQ == 0
    tq = ATT_TQ
    n_q = SEQ // tq
    kern = functools.partial(_attn_kernel, lambda_init=lambda_init)
    blk = 2 * tq * V_DIM * 2 + 2 * SEQ * V_DIM * 2 + 4 * HEAD_DIM * 4 + V_DIM * 4
    scr = 2 * tq * V_DIM * 4 + 4 * tq * V7X_LANES * 4
    return pl.pallas_call(
        kern,
        out_shape=jax.ShapeDtypeStruct((N_TOK, D_MODEL), BF16),
        grid=(BATCH, N_HEADS, n_q),
        in_specs=[
            pl.BlockSpec((tq, V_DIM), lambda b, h, i: (b * n_q + i, h)),
            pl.BlockSpec((SEQ, V_DIM), lambda b, h, i: (b, N_HEADS + h)),
            pl.BlockSpec((SEQ, V_DIM), lambda b, h, i: (b, 2 * N_HEADS + h)),
            pl.BlockSpec((4, HEAD_DIM), lambda b, h, i: (0, 0)),
            pl.BlockSpec((1, V_DIM), lambda b, h, i: (0, 0)),
        ],
        out_specs=pl.BlockSpec((tq, V_DIM), lambda b, h, i: (b * n_q + i, h)),
        scratch_shapes=[pltpu.VMEM((2, tq, V7X_LANES), F32), pltpu.VMEM((2, tq, V7X_LANES), F32),
                        pltpu.VMEM((2, tq, V_DIM), F32)],
        compiler_params=_cparams(("parallel", "parallel", "arbitrary"),
                                 _vmem_limit(blk, scr, 8 * tq * ATT_TK * 4)),
        name="diff_attn",
    )(qkv, qkv, qkv, lam_vecs, subln_g)


def _wo_kernel(a_ref, w_ref, x_ref, mod_ref, lng_ref, lnb_ref, wr_ref,
               o_ref, h_ref, lg_ref):
    k = pl.program_id(1)
    tm = x_ref.shape[0]

    @pl.when(k == 0)
    def _():
        def prep(rows):
            o_ref[rows, :] = jnp.zeros((EPI_ROWS, o_ref.shape[1]), F32)
        _for_row_chunks(tm, prep)

    _accumulate_dot(o_ref, a_ref[...], w_ref)

    @pl.when(k == pl.num_programs(1) - 1)
    def _():
        def fin(rows):
            z = ALPHA * x_ref[rows, :] + (1.0 + mod_ref[2:3, :]) * o_ref[rows, :]
            xn = _layer_norm(z, lng_ref[...], lnb_ref[...])
            o_ref[rows, :] = xn
            h = xn * (1.0 + mod_ref[4:5, :]) + mod_ref[3:4, :]
            h_ref[rows, :] = h.astype(BF16)
            lane = lax.broadcasted_iota(jnp.int32, (EPI_ROWS, ROUTER_PAD), 1)
            lg = jnp.zeros((EPI_ROWS, ROUTER_PAD), F32)
            for e in range(N_EXPERTS):
                val = jnp.sum(h * wr_ref[e:e + 1, :], axis=1, keepdims=True)
                lg = jnp.where(lane == e, val, lg)
            lg_ref[rows, :] = lg
        _for_row_chunks(tm, fin)


def _wo_call(attn, w_o, x, mod, lng, lnb, w_router_t):
    tm, tk, d = WO_TM, WO_TK, D_MODEL
    tiles_per_seq = SEQ // tm
    blk = (tm * tk * 2 + tk * d * 2 + tm * d * 4 * 2 + tm * d * 2 + ADA_CHUNKS * d * 4
           + 2 * d * 4 + tm * ROUTER_PAD * 4)
    return pl.pallas_call(
        _wo_kernel,
        out_shape=(jax.ShapeDtypeStruct((N_TOK, d), F32),
                   jax.ShapeDtypeStruct((N_TOK, d), BF16),
                   jax.ShapeDtypeStruct((N_TOK, ROUTER_PAD), F32)),
        grid=(N_TOK // tm, d // tk),
        in_specs=[
            pl.BlockSpec((tm, tk), lambda i, k: (i, k)),
            pl.BlockSpec((tk, d), lambda i, k: (k, 0)),
            pl.BlockSpec((tm, d), lambda i, k: (i, 0)),
            pl.BlockSpec((None, ADA_CHUNKS, d), lambda i, k: (i // tiles_per_seq, 0, 0)),
            pl.BlockSpec((1, d), lambda i, k: (0, 0)),
            pl.BlockSpec((1, d), lambda i, k: (0, 0)),
            pl.BlockSpec((N_EXPERTS, d), lambda i, k: (0, 0)),
        ],
        out_specs=(pl.BlockSpec((tm, d), lambda i, k: (i, 0)),
                   pl.BlockSpec((tm, d), lambda i, k: (i, 0)),
                   pl.BlockSpec((tm, ROUTER_PAD), lambda i, k: (i, 0))),
        compiler_params=_cparams(("parallel", "arbitrary"),
                                 _vmem_limit(blk, 2 * N_EXPERTS * d * 4,
                                             tm * ACC_TN * 4 + 6 * EPI_ROWS * d * 4)),
        name="attn_out_proj",
    )(attn, w_o, x, mod, lng, lnb, w_router_t)


def _moe_kernel(be_ref, nu_ref, h_ref, rw_ref, wg_ref, wu_ref, wd_ref, o_ref):
    i = pl.program_id(0)
    k = pl.program_id(1)
    tm = h_ref.shape[0]

    @pl.when(i < nu_ref[0])
    def _():
        @pl.when(k == 0)
        def _():
            def prep(rows):
                o_ref[rows, :] = jnp.zeros((EPI_ROWS, o_ref.shape[1]), F32)
            _for_row_chunks(tm, prep)

        h = h_ref[...]
        g = jnp.dot(h, wg_ref[...], preferred_element_type=F32)
        u = jnp.dot(h, wu_ref[...], preferred_element_type=F32)
        _accumulate_dot(o_ref, (_silu(g) * u).astype(BF16), wd_ref)

        @pl.when(k == pl.num_programs(1) - 1)
        def _():
            def fin(rows):
                o_ref[rows, :] = o_ref[rows, :] * rw_ref[rows, :]
            _for_row_chunks(tm, fin)


def _moe_call(blk_expert, n_used, h_rows, row_w, w_gu, w_down):
    tm, tf, d = MOE_TM, MOE_TF, D_MODEL
    n_f = D_FF_EXPERT // tf

    def row_blk(i, nu):
        return jnp.minimum(i, nu[0] - 1)

    def f_blk(i, k, nu):
        return jnp.where(i < nu[0], k, n_f - 1)

    blk = tm * d * 2 + tm * V7X_LANES * 4 + 3 * d * tf * 2 + tm * d * 4
    grid_spec = pltpu.PrefetchScalarGridSpec(
        num_scalar_prefetch=2,
        grid=(MOE_N_BLK, n_f),
        in_specs=[
            pl.BlockSpec((tm, d), lambda i, k, be, nu: (row_blk(i, nu), 0)),
            pl.BlockSpec((tm, 1), lambda i, k, be, nu: (row_blk(i, nu), 0)),
            pl.BlockSpec((None, d, tf),
                         lambda i, k, be, nu: (be[row_blk(i, nu)], 0, f_blk(i, k, nu))),
            pl.BlockSpec((None, d, tf),
                         lambda i, k, be, nu: (be[row_blk(i, nu)], 0, f_blk(i, k, nu) + n_f)),
            pl.BlockSpec((None, tf, d),
                         lambda i, k, be, nu: (be[row_blk(i, nu)], f_blk(i, k, nu), 0)),
        ],
        out_specs=pl.BlockSpec((tm, d), lambda i, k, be, nu: (row_blk(i, nu), 0)),
    )
    return pl.pallas_call(
        _moe_kernel,
        out_shape=jax.ShapeDtypeStruct((MOE_N_ROWS, d), F32),
        grid_spec=grid_spec,
        compiler_params=_cparams(("arbitrary", "arbitrary"),
                                 _vmem_limit(blk, 0, 6 * tm * tf * 4 + tm * ACC_TN * 4)),
        name="moe_ffn",
    )(blk_expert, n_used, h_rows, row_w, w_gu, w_gu, w_down)


def _final_kernel(x_ref, y0_ref, y1_ref, mod_ref, lng_ref, lnb_ref, o_ref):
    def fin(rows):
        y = y0_ref[rows, :] + y1_ref[rows, :]
        z = ALPHA * x_ref[rows, :] + (1.0 + mod_ref[5:6, :]) * y
        o_ref[rows, :] = _layer_norm(z, lng_ref[...], lnb_ref[...])
    _for_row_chunks(x_ref.shape[0], fin)


def _final_call(x, y0, y1, mod, lng, lnb):
    tm, d = FIN_TM, D_MODEL
    tiles_per_seq = SEQ // tm
    blk = 4 * tm * d * 4 + ADA_CHUNKS * d * 4 + 2 * d * 4
    return pl.pallas_call(
        _final_kernel,
        out_shape=jax.ShapeDtypeStruct((N_TOK, d), F32),
        grid=(N_TOK // tm,),
        in_specs=[
            pl.BlockSpec((tm, d), lambda i: (i, 0)),
            pl.BlockSpec((tm, d), lambda i: (i, 0)),
            pl.BlockSpec((tm, d), lambda i: (i, 0)),
            pl.BlockSpec((None, ADA_CHUNKS, d), lambda i: (i // tiles_per_seq, 0, 0)),
            pl.BlockSpec((1, d), lambda i: (0, 0)),
            pl.BlockSpec((1, d), lambda i: (0, 0)),
        ],
        out_specs=pl.BlockSpec((tm, d), lambda i: (i, 0)),
        compiler_params=_cparams(("parallel",), _vmem_limit(blk, 0, 8 * EPI_ROWS * d * 4)),
        name="final_ln",
    )(x, y0, y1, mod, lng, lnb)


def _route(logits):
    top_logit, top_idx = lax.top_k(logits, TOP_K)
    top_w = jax.nn.softmax(top_logit, axis=-1)
    e_flat = top_idx.reshape(MOE_N_ASG).astype(jnp.int32)
    w_flat = top_w.reshape(MOE_N_ASG)
    tok_flat = jnp.arange(MOE_N_ASG, dtype=jnp.int32) // TOP_K
    onehot = (e_flat[:, None] == jnp.arange(N_EXPERTS, dtype=jnp.int32)[None, :]).astype(jnp.int32)
    csum = jnp.cumsum(onehot, axis=0)
    rank = jnp.sum((csum - onehot) * onehot, axis=1)
    counts = csum[-1]
    padded = (counts + MOE_TM - 1) // MOE_TM * MOE_TM
    padded_ends = jnp.cumsum(padded)
    padded_starts = padded_ends - padded
    dest = padded_starts[e_flat] + rank
    n_used = (padded_ends[-1:] // MOE_TM).astype(jnp.int32)
    blk_start = jnp.arange(MOE_N_BLK, dtype=jnp.int32) * MOE_TM
    blk_expert = jnp.minimum(jnp.searchsorted(padded_ends, blk_start, side='right'),
                             N_EXPERTS - 1).astype(jnp.int32)
    row_tok = jnp.zeros((MOE_N_ROWS,), jnp.int32).at[dest].set(tok_flat)
    row_w = jnp.zeros((MOE_N_ROWS,), F32).at[dest].set(w_flat)
    return blk_expert, n_used, dest, row_tok, row_w


def _rope_tables(positions):
    inv_freq = ROPE_THETA ** (-jnp.arange(0, ROT_DIM, 2, dtype=F32) / ROT_DIM)
    ang = positions.astype(F32)[..., None] * inv_freq
    cos = jnp.cos(ang).reshape(N_TOK, ROT_HALF)
    sin = jnp.sin(ang).reshape(N_TOK, ROT_HALF)
    rest = HEAD_DIM - ROT_DIM
    c = jnp.concatenate([cos, cos, jnp.ones((N_TOK, rest), F32)], axis=1)
    s_up = jnp.concatenate([jnp.zeros((N_TOK, ROT_HALF), F32), sin,
                            jnp.zeros((N_TOK, rest), F32)], axis=1)
    s_dn = jnp.concatenate([-sin, jnp.zeros((N_TOK, HEAD_DIM - ROT_HALF), F32)], axis=1)
    return c, s_up, s_dn


def kernel(x, c, positions, ada_w, ada_b, ln_g, ln_b, kv_ada_w, kv_ada_b, w_pool, pool_scale,
           w_kv, w_q, w_o, lam_q1, lam_k1, lam_q2, lam_k2, subln_g, ffn_w_gu, ffn_w_down,
           router_w, moe_w_gu, moe_w_down):
    d = D_MODEL
    xt = x.reshape(N_TOK, d)

    c8 = jnp.pad(c, ((0, V7X_SUBLANES - BATCH), (0, 0)))
    mods = _ada_call(c8, ada_w, ada_b)[:, :BATCH].reshape(DEPTH, BATCH, ADA_CHUNKS, d)
    mod_kv = _ada_call(c8, kv_ada_w[None], kv_ada_b[None])[0, :BATCH].reshape(BATCH, 2, d)

    x1 = _pool_call(xt, mods[0], w_pool[0].astype(BF16), pool_scale[0][None],
                    ln_g[0, 0][None], ln_b[0, 0][None])
    x2 = _ffn_call(x1, mods[0], ffn_w_gu[0].astype(BF16), ffn_w_down[0].astype(BF16),
                   ln_g[0, 1][None], ln_b[0, 1][None])

    rope_c, rope_up, rope_dn = _rope_tables(positions)
    w_qkv = jnp.concatenate([w_q[0].astype(BF16), w_kv.astype(BF16)], axis=1)
    qkv = _qkv_call(x2, mods[1], mod_kv, w_qkv, rope_c, rope_up, rope_dn)
    lam_vecs = jnp.stack([lam_q1[0], lam_k1[0], lam_q2[0], lam_k2[0]], axis=0)
    lambda_init = 0.8 - 0.6 * math.exp(-0.3 * 1)
    attn = _attn_call(qkv, lam_vecs, subln_g[0][None], lambda_init)
    w_router_t = router_w[0].T
    x3, h3, logits = _wo_call(attn, w_o[0].astype(BF16), x2, mods[1],
                              ln_g[1, 0][None], ln_b[1, 0][None], w_router_t)

    blk_expert, n_used, dest, row_tok, row_w = _route(logits[:, :N_EXPERTS])
    h_rows = h3.at[row_tok].get(mode='promise_in_bounds')
    y_rows = _moe_call(blk_expert, n_used, h_rows, row_w[:, None],
                       moe_w_gu[0].astype(BF16), moe_w_down[0].astype(BF16))
    dest = dest.reshape(N_TOK, TOP_K)
    y0 = y_rows.at[dest[:, 0]].get(mode='promise_in_bounds')
    y1 = y_rows.at[dest[:, 1]].get(mode='promise_in_bounds')
    out = _final_call(x3, y0, y1, mods[1], ln_g[1, 1][None], ln_b[1, 1][None])
    return out.reshape(BATCH, SEQ, d)
```

```python
import functools
import math

import jax
import jax.numpy as jnp
from jax import lax
from jax.experimental import pallas as pl
from jax.experimental.pallas import tpu as pltpu

F32 = jnp.float32
BF16 = jnp.bfloat16

D_MODEL = 4096
BATCH = 2
SEQ = 8192
N_TOK = BATCH * SEQ
DEPTH = 2
POOL_WINDOWS = (2, 4, 8, 16)
POOL_GROUP_DIM = D_MODEL // len(POOL_WINDOWS)
POOL_HALO = 16
N_HEADS = 16
HEAD_DIM = 128
V_DIM = 2 * HEAD_DIM
ROT_DIM = HEAD_DIM // 4
ROT_HALF = ROT_DIM // 2
ROPE_THETA = 500000.0
D_FF = 11008
N_EXPERTS = 8
TOP_K = 2
D_FF_EXPERT = D_MODEL
LN_EPS = 1e-5
ALPHA = (2.0 * DEPTH) ** 0.25
ADA_CHUNKS = 6
LOG2E = math.log2(math.e)
Q_PRESCALE = HEAD_DIM ** -0.5 * LOG2E

V7X_VMEM_BYTES = 64 * 1024 * 1024
V7X_LANES = 128
V7X_SUBLANES = 8

ADA_TN = 1024
ADA_TK = 512
POOL_TS = 256
FFN_TM = 512
FFN_TF = 256
QKV_TM = 512
QKV_TN = 1024
ATT_TQ = 512
ATT_TK = 1024
WO_TM = 512
WO_TK = 512
MOE_TM = 512
MOE_TF = 512
MOE_N_F = D_FF_EXPERT // MOE_TF
FIN_TM = 256
ACC_TN = 1024
EPI_ROWS = 64
DMA_UNROLL = 8
ROUTER_PAD = V7X_LANES

MOE_N_ASG = N_TOK * TOP_K
MOE_N_BLK = MOE_N_ASG // MOE_TM + N_EXPERTS
MOE_N_ROWS = MOE_N_BLK * MOE_TM


def _vmem_limit(pipelined_bytes, resident_bytes=0, temp_bytes=0):
    need = 2 * pipelined_bytes + resident_bytes + temp_bytes + (2 << 20)
    assert need <= V7X_VMEM_BYTES - (2 << 20), need
    return int(need)


def _cparams(sem, vmem):
    return pltpu.CompilerParams(dimension_semantics=sem, vmem_limit_bytes=vmem)


def _layer_norm(z, g, b):
    mu = jnp.mean(z, axis=-1, keepdims=True)
    zc = z - mu
    var = jnp.mean(zc * zc, axis=-1, keepdims=True)
    return zc * lax.rsqrt(var + LN_EPS) * g + b


def _silu(x):
    return x / (1.0 + jnp.exp(-x))


def _for_row_chunks(n_rows, fn):
    def body(r, carry):
        fn(pl.ds(pl.multiple_of(r * EPI_ROWS, EPI_ROWS), EPI_ROWS))
        return carry
    lax.fori_loop(0, n_rows // EPI_ROWS, body, 0)


def _accumulate_dot(o_ref, a, w_ref):
    for j in range(o_ref.shape[1] // ACC_TN):
        cols = slice(j * ACC_TN, (j + 1) * ACC_TN)
        o_ref[:, cols] += jnp.dot(a, w_ref[:, cols], preferred_element_type=F32)


def _ada_kernel(c_ref, w_ref, b_ref, o_ref):
    acc = jnp.zeros(o_ref.shape, F32) + b_ref[...]
    for kk in range(c_ref.shape[1] // ADA_TK):
        ks = slice(kk * ADA_TK, (kk + 1) * ADA_TK)
        cond = _silu(c_ref[:, ks]).astype(BF16)
        acc = acc + jnp.dot(cond, w_ref[ks, :].astype(BF16), preferred_element_type=F32)
    o_ref[...] = acc


def _ada_call(c8, w, b):
    n_l, d, n = w.shape
    tn = ADA_TN
    blk = d * tn * 4 + 8 * d * 4 + 8 * tn * 4 + tn * 4
    return pl.pallas_call(
        _ada_kernel,
        out_shape=jax.ShapeDtypeStruct((n_l, 8, n), F32),
        grid=(n_l, n // tn),
        in_specs=[
            pl.BlockSpec((8, d), lambda l, j: (0, 0)),
            pl.BlockSpec((None, d, tn), lambda l, j: (l, 0, j)),
            pl.BlockSpec((None, 1, tn), lambda l, j: (l, 0, j)),
        ],
        out_specs=pl.BlockSpec((None, 8, tn), lambda l, j: (l, 0, j)),
        compiler_params=_cparams(("parallel", "parallel"),
                                 _vmem_limit(blk, temp_bytes=4 * ADA_TK * tn * 4)),
        name="ada_mod",
    )(c8, w, b.reshape(n_l, 1, n))


def _pool_kernel(x_ref, halo_ref, mod_ref, wp_ref, ps_ref, lng_ref, lnb_ref, o_ref):
    i = pl.program_id(1)
    ts = x_ref.shape[0]
    t1 = (i * ts + 1 + lax.broadcasted_iota(jnp.int32, (ts, 1), 0)).astype(F32)
    for g, w in enumerate(POOL_WINDOWS):
        cols = slice(g * POOL_GROUP_DIM, (g + 1) * POOL_GROUP_DIM)
        sh = mod_ref[0:1, cols]
        sc = mod_ref[1:2, cols]
        gate = mod_ref[2:3, cols]
        x = x_ref[:, cols]
        h = x * (1.0 + sc) + sh
        hh = jnp.where(i > 0, halo_ref[:, cols] * (1.0 + sc) + sh, 0.0)
        s = jnp.concatenate([hh, h], axis=0)
        span = 1
        while span < w:
            s = s + pltpu.roll(s, span, 0)
            span *= 2
        win = s[POOL_HALO:, :]
        pooled = win / jnp.minimum(t1, float(w)) - h
        mixed = jnp.dot(pooled.astype(BF16), wp_ref[g], preferred_element_type=F32)
        o_ref[:, cols] = ALPHA * x + (1.0 + gate) * (mixed * ps_ref[:, cols])

    def ln_rows(rows):
        o_ref[rows, :] = _layer_norm(o_ref[rows, :], lng_ref[...], lnb_ref[...])
    _for_row_chunks(ts, ln_rows)


def _pool_call(x, mod, wp, ps, lng, lnb):
    ts = POOL_TS
    n_s = SEQ // ts
    halo_per_tile = ts // POOL_HALO
    d = D_MODEL
    blk = ts * d * 4 * 2 + POOL_HALO * d * 4 + ADA_CHUNKS * d * 4 + 3 * d * 4
    return pl.pallas_call(
        _pool_kernel,
        out_shape=jax.ShapeDtypeStruct((N_TOK, d), F32),
        grid=(BATCH, n_s),
        in_specs=[
            pl.BlockSpec((ts, d), lambda b, i: (b * n_s + i, 0)),
            pl.BlockSpec((POOL_HALO, d),
                         lambda b, i: (jnp.maximum((b * n_s + i) * halo_per_tile - 1, 0), 0)),
            pl.BlockSpec((None, ADA_CHUNKS, d), lambda b, i: (b, 0, 0)),
            pl.BlockSpec(wp.shape, lambda b, i: (0, 0, 0), pipeline_mode=pl.Buffered(1)),
            pl.BlockSpec((1, d), lambda b, i: (0, 0)),
            pl.BlockSpec((1, d), lambda b, i: (0, 0)),
            pl.BlockSpec((1, d), lambda b, i: (0, 0)),
        ],
        out_specs=pl.BlockSpec((ts, d), lambda b, i: (b * n_s + i, 0)),
        compiler_params=_cparams(
            ("parallel", "parallel"),
            _vmem_limit(blk, wp.size * 2, 8 * (ts + POOL_HALO) * POOL_GROUP_DIM * 4)),
        name="pool_mixer",
    )(x, x, mod, wp, ps, lng, lnb)


def _ffn_kernel(x_ref, mod_ref, wg_ref, wu_ref, wd_ref, lng_ref, lnb_ref, o_ref, h_scr):
    k = pl.program_id(1)
    tm = x_ref.shape[0]

    @pl.when(k == 0)
    def _():
        def prep(rows):
            h_scr[rows, :] = (x_ref[rows, :] * (1.0 + mod_ref[4:5, :])
                              + mod_ref[3:4, :]).astype(BF16)
            o_ref[rows, :] = jnp.zeros((EPI_ROWS, o_ref.shape[1]), F32)
        _for_row_chunks(tm, prep)

    h = h_scr[...]
    g = jnp.dot(h, wg_ref[...], preferred_element_type=F32)
    u = jnp.dot(h, wu_ref[...], preferred_element_type=F32)
    _accumulate_dot(o_ref, (_silu(g) * u).astype(BF16), wd_ref)

    @pl.when(k == pl.num_programs(1) - 1)
    def _():
        def fin(rows):
            z = ALPHA * x_ref[rows, :] + (1.0 + mod_ref[5:6, :]) * o_ref[rows, :]
            o_ref[rows, :] = _layer_norm(z, lng_ref[...], lnb_ref[...])
        _for_row_chunks(tm, fin)


def _ffn_call(x, mod, w_gu, w_down, lng, lnb):
    tm, tf, d = FFN_TM, FFN_TF, D_MODEL
    n_f = D_FF // tf
    tiles_per_seq = SEQ // tm
    blk = tm * d * 4 * 2 + ADA_CHUNKS * d * 4 + 3 * d * tf * 2 + 2 * d * 4
    return pl.pallas_call(
        _ffn_kernel,
        out_shape=jax.ShapeDtypeStruct((N_TOK, d), F32),
        grid=(N_TOK // tm, n_f),
        in_specs=[
            pl.BlockSpec((tm, d), lambda i, k: (i, 0)),
            pl.BlockSpec((None, ADA_CHUNKS, d), lambda i, k: (i // tiles_per_seq, 0, 0)),
            pl.BlockSpec((d, tf), lambda i, k: (0, k)),
            pl.BlockSpec((d, tf), lambda i, k: (0, k + n_f)),
            pl.BlockSpec((tf, d), lambda i, k: (k, 0)),
            pl.BlockSpec((1, d), lambda i, k: (0, 0)),
            pl.BlockSpec((1, d), lambda i, k: (0, 0)),
        ],
        out_specs=pl.BlockSpec((tm, d), lambda i, k: (i, 0)),
        scratch_shapes=[pltpu.VMEM((tm, d), BF16)],
        compiler_params=_cparams(("parallel", "arbitrary"),
                                 _vmem_limit(blk, tm * d * 2, 6 * tm * tf * 4 + tm * ACC_TN * 4)),
        name="dense_ffn",
    )(x, mod, w_gu, w_gu, w_down, lng, lnb)


def _rope(r, c, s_up, s_dn):
    outs = []
    for j in range(r.shape[1] // HEAD_DIM):
        xc = r[:, j * HEAD_DIM:(j + 1) * HEAD_DIM]
        outs.append(xc * c + pltpu.roll(xc, ROT_HALF, 1) * s_up
                    + pltpu.roll(xc, HEAD_DIM - ROT_HALF, 1) * s_dn)
    return jnp.concatenate(outs, axis=1)


def _qkv_kernel(x_ref, modm_ref, modkv_ref, w_ref, c_ref, su_ref, sd_ref, o_ref,
                hq_scr, hkv_scr, *, n_q, n_k):
    n = pl.program_id(1)

    @pl.when(n == 0)
    def _():
        def prep(rows):
            x = x_ref[rows, :]
            hq_scr[rows, :] = (x * (1.0 + modm_ref[1:2, :]) + modm_ref[0:1, :]).astype(BF16)
            hkv_scr[rows, :] = (x * (1.0 + modkv_ref[1:2, :]) + modkv_ref[0:1, :]).astype(BF16)
        _for_row_chunks(x_ref.shape[0], prep)

    @pl.when(n < n_q)
    def _():
        r = jnp.dot(hq_scr[...], w_ref[...], preferred_element_type=F32)
        r = _rope(r, c_ref[...], su_ref[...], sd_ref[...]) * Q_PRESCALE
        o_ref[...] = r.astype(BF16)

    @pl.when((n >= n_q) & (n < n_q + n_k))
    def _():
        r = jnp.dot(hkv_scr[...], w_ref[...], preferred_element_type=F32)
        o_ref[...] = _rope(r, c_ref[...], su_ref[...], sd_ref[...]).astype(BF16)

    @pl.when(n >= n_q + n_k)
    def _():
        o_ref[...] = jnp.dot(hkv_scr[...], w_ref[...],
                             preferred_element_type=F32).astype(BF16)


def _qkv_call(x, modm, modkv, w_qkv, rope_c, rope_up, rope_dn):
    tm, tn, d = QKV_TM, QKV_TN, D_MODEL
    n_out = w_qkv.shape[1]
    tiles_per_seq = SEQ // tm
    blk = tm * d * 4 + (ADA_CHUNKS + 2) * d * 4 + d * tn * 2 + 3 * tm * HEAD_DIM * 4 + tm * tn * 2
    kern = functools.partial(_qkv_kernel, n_q=D_MODEL // tn, n_k=D_MODEL // tn)
    return pl.pallas_call(
        kern,
        out_shape=jax.ShapeDtypeStruct((N_TOK, n_out), BF16),
        grid=(N_TOK // tm, n_out // tn),
        in_specs=[
            pl.BlockSpec((tm, d), lambda i, n: (i, 0)),
            pl.BlockSpec((None, ADA_CHUNKS, d), lambda i, n: (i // tiles_per_seq, 0, 0)),
            pl.BlockSpec((None, 2, d), lambda i, n: (i // tiles_per_seq, 0, 0)),
            pl.BlockSpec((d, tn), lambda i, n: (0, n)),
            pl.BlockSpec((tm, HEAD_DIM), lambda i, n: (i, 0)),
            pl.BlockSpec((tm, HEAD_DIM), lambda i, n: (i, 0)),
            pl.BlockSpec((tm, HEAD_DIM), lambda i, n: (i, 0)),
        ],
        out_specs=pl.BlockSpec((tm, tn), lambda i, n: (i, n)),
        scratch_shapes=[pltpu.VMEM((tm, d), BF16), pltpu.VMEM((tm, d), BF16)],
        compiler_params=_cparams(("parallel", "arbitrary"),
                                 _vmem_limit(blk, 2 * tm * d * 2, 4 * tm * tn * 4)),
        name="qkv_proj",
    )(x, modm, modkv, w_qkv, rope_c, rope_up, rope_dn)


def _attn_kernel(q_ref, k_ref, v_ref, lam_ref, g_ref, o_ref, m_scr, l_scr, acc_scr,
                 *, lambda_init):
    i = pl.program_id(2)
    tq = q_ref.shape[0]
    tk = ATT_TK
    m_scr[...] = jnp.full(m_scr.shape, -jnp.inf, F32)
    l_scr[...] = jnp.zeros(l_scr.shape, F32)
    acc_scr[...] = jnp.zeros(acc_scr.shape, F32)

    def chunk(start, width, masked):
        kc = k_ref[pl.ds(start, width), :]
        vc = v_ref[pl.ds(start, width), :]
        for c in range(2):
            qc = q_ref[:, c * HEAD_DIM:(c + 1) * HEAD_DIM]
            s = lax.dot_general(qc, kc[:, c * HEAD_DIM:(c + 1) * HEAD_DIM],
                                (((1,), (1,)), ((), ())), preferred_element_type=F32)
            if masked:
                row = lax.broadcasted_iota(jnp.int32, (tq, width), 0)
                col = lax.broadcasted_iota(jnp.int32, (tq, width), 1)
                s = jnp.where(col <= row, s, -jnp.inf)
            lanes = [s[:, t * V7X_LANES:(t + 1) * V7X_LANES] for t in range(width // V7X_LANES)]
            m_prev = m_scr[c]
            m_new = jnp.maximum(m_prev, jnp.max(functools.reduce(jnp.maximum, lanes),
                                                axis=1, keepdims=True))
            a = jnp.exp2(m_prev - m_new)
            ps = [jnp.exp2(x - m_new) for x in lanes]
            l_scr[c] = a * l_scr[c] + functools.reduce(jnp.add, ps)
            p = jnp.concatenate([x.astype(BF16) for x in ps], axis=1)
            acc_scr[c] = (jnp.concatenate([a, a], axis=1) * acc_scr[c]
                          + jnp.dot(p, vc, preferred_element_type=F32))
            m_scr[c] = m_new

    per_wide = tk // tq
    n_wide = i // per_wide

    def body(j, carry):
        chunk(pl.multiple_of(2 * j * tk, tk), tk, False)
        chunk(pl.multiple_of((2 * j + 1) * tk, tk), tk, False)
        return carry

    lax.fori_loop(0, n_wide // 2, body, 0)

    @pl.when(n_wide % 2 == 1)
    def _():
        chunk(pl.multiple_of((n_wide - 1) * tk, tk), tk, False)

    for r in range(per_wide - 1):
        @pl.when(i - n_wide * per_wide > r)
        def _():
            chunk(pl.multiple_of((n_wide * per_wide + r) * tq, tq), tq, False)
    chunk(pl.multiple_of(i * tq, tq), tq, True)

    lam = (jnp.exp(jnp.sum(lam_ref[0:1, :] * lam_ref[1:2, :], axis=1, keepdims=True))
           - jnp.exp(jnp.sum(lam_ref[2:3, :] * lam_ref[3:4, :], axis=1, keepdims=True))
           + lambda_init)
    l0 = jnp.sum(l_scr[0], axis=1, keepdims=True)
    l1 = jnp.sum(l_scr[1], axis=1, keepdims=True)
    o = acc_scr[0] / l0 - lam * (acc_scr[1] / l1)
    ms = jnp.mean(o * o, axis=-1, keepdims=True)
    o_ref[...] = (o * lax.rsqrt(ms + LN_EPS) * g_ref[...] * (1.0 - lambda_init)).astype(BF16)


def _attn_call(qkv, lam_vecs, subln_g, lambda_init):
    assert ATT_TK % ATT_TQ == 0
    tq = ATT_TQ
    n_q = SEQ // tq
    kern = functools.partial(_attn_kernel, lambda_init=lambda_init)
    blk = 2 * tq * V_DIM * 2 + 2 * SEQ * V_DIM * 2 + 4 * HEAD_DIM * 4 + V_DIM * 4
    scr = 2 * tq * V_DIM * 4 + 4 * tq * V7X_LANES * 4
    return pl.pallas_call(
        kern,
        out_shape=jax.ShapeDtypeStruct((N_TOK, D_MODEL), BF16),
        grid=(BATCH, N_HEADS, n_q),
        in_specs=[
            pl.BlockSpec((tq, V_DIM), lambda b, h, i: (b * n_q + i, h)),
            pl.BlockSpec((SEQ, V_DIM), lambda b, h, i: (b, N_HEADS + h)),
            pl.BlockSpec((SEQ, V_DIM), lambda b, h, i: (b, 2 * N_HEADS + h)),
            pl.BlockSpec((4, HEAD_DIM), lambda b, h, i: (0, 0)),
            pl.BlockSpec((1, V_DIM), lambda b, h, i: (0, 0)),
        ],
        out_specs=pl.BlockSpec((tq, V_DIM), lambda b, h, i: (b * n_q + i, h)),
        scratch_shapes=[pltpu.VMEM((2, tq, V7X_LANES), F32), pltpu.VMEM((2, tq, V7X_LANES), F32),
                        pltpu.VMEM((2, tq, V_DIM), F32)],
        compiler_params=_cparams(("parallel", "parallel", "arbitrary"),
                                 _vmem_limit(blk, scr, 8 * tq * ATT_TK * 4)),
        name="diff_attn",
    )(qkv, qkv, qkv, lam_vecs, subln_g)


def _wo_kernel(a_ref, w_ref, x_ref, mod_ref, lng_ref, lnb_ref, wr_ref,
               o_ref, h_ref, lg_ref):
    k = pl.program_id(1)
    tm = x_ref.shape[0]

    @pl.when(k == 0)
    def _():
        def prep(rows):
            o_ref[rows, :] = jnp.zeros((EPI_ROWS, o_ref.shape[1]), F32)
        _for_row_chunks(tm, prep)

    _accumulate_dot(o_ref, a_ref[...], w_ref)

    @pl.when(k == pl.num_programs(1) - 1)
    def _():
        def fin(rows):
            z = ALPHA * x_ref[rows, :] + (1.0 + mod_ref[2:3, :]) * o_ref[rows, :]
            xn = _layer_norm(z, lng_ref[...], lnb_ref[...])
            o_ref[rows, :] = xn
            h = xn * (1.0 + mod_ref[4:5, :]) + mod_ref[3:4, :]
            bits = lax.bitcast_convert_type(h.astype(BF16).astype(F32), jnp.uint32)
            half = bits.shape[1] // 2
            h_ref[rows, :] = bits[:, :half] | (bits[:, half:] >> 16)
            lane = lax.broadcasted_iota(jnp.int32, (EPI_ROWS, ROUTER_PAD), 1)
            lg = jnp.zeros((EPI_ROWS, ROUTER_PAD), F32)
            for e in range(N_EXPERTS):
                val = jnp.sum(h * wr_ref[e:e + 1, :], axis=1, keepdims=True)
                lg = jnp.where(lane == e, val, lg)
            lg_ref[rows, :] = lg
        _for_row_chunks(tm, fin)


def _wo_call(attn, w_o, x, mod, lng, lnb, w_router_t):
    tm, tk, d = WO_TM, WO_TK, D_MODEL
    tiles_per_seq = SEQ // tm
    blk = (tm * tk * 2 + tk * d * 2 + tm * d * 4 * 2 + tm * d * 2 + ADA_CHUNKS * d * 4
           + 2 * d * 4 + tm * ROUTER_PAD * 4)
    return pl.pallas_call(
        _wo_kernel,
        out_shape=(jax.ShapeDtypeStruct((N_TOK, d), F32),
                   jax.ShapeDtypeStruct((N_TOK, d // 2), jnp.uint32),
                   jax.ShapeDtypeStruct((N_TOK, ROUTER_PAD), F32)),
        grid=(N_TOK // tm, d // tk),
        in_specs=[
            pl.BlockSpec((tm, tk), lambda i, k: (i, k)),
            pl.BlockSpec((tk, d), lambda i, k: (k, 0)),
            pl.BlockSpec((tm, d), lambda i, k: (i, 0)),
            pl.BlockSpec((None, ADA_CHUNKS, d), lambda i, k: (i // tiles_per_seq, 0, 0)),
            pl.BlockSpec((1, d), lambda i, k: (0, 0)),
            pl.BlockSpec((1, d), lambda i, k: (0, 0)),
            pl.BlockSpec((N_EXPERTS, d), lambda i, k: (0, 0)),
        ],
        out_specs=(pl.BlockSpec((tm, d), lambda i, k: (i, 0)),
                   pl.BlockSpec((tm, d // 2), lambda i, k: (i, 0)),
                   pl.BlockSpec((tm, ROUTER_PAD), lambda i, k: (i, 0))),
        compiler_params=_cparams(("parallel", "arbitrary"),
                                 _vmem_limit(blk, 2 * N_EXPERTS * d * 4,
                                             tm * ACC_TN * 4 + 6 * EPI_ROWS * d * 4)),
        name="attn_out_proj",
    )(attn, w_o, x, mod, lng, lnb, w_router_t)


def _gather_row_copy(src_hbm, row, dst_buf, dst_row, sem):
    return pltpu.make_async_copy(src_hbm.at[pl.ds(row, 1), :], dst_buf.at[pl.ds(dst_row, 1), :], sem)


def _moe_kernel(be_ref, nu_ref, rt_ref, h_hbm, rw_ref, wg_ref, wu_ref, wd_ref, o_ref,
                hbuf, h_scr, sem):
    i = pl.program_id(0)
    k = pl.program_id(1)
    n_f = pl.num_programs(1)
    tm = h_scr.shape[0]
    half = h_scr.shape[1] // 2
    per_step = tm // MOE_N_F
    n_used = nu_ref[0]

    def start_rows(blk, first, count):
        def body(r, carry):
            row = first + r
            _gather_row_copy(h_hbm, rt_ref[blk * tm + row], hbuf, row, sem.at[0]).start()
            return carry
        lax.fori_loop(0, count, body, 0, unroll=DMA_UNROLL)

    def wait_rows():
        def body(r, carry):
            _gather_row_copy(h_hbm, 0, hbuf, r, sem.at[0]).wait()
            return carry
        lax.fori_loop(0, tm, body, 0, unroll=DMA_UNROLL)

    @pl.when((i >= n_used) & (k == 0))
    def _():
        def clear(rows):
            o_ref[rows, :] = jnp.zeros((EPI_ROWS, o_ref.shape[1]), F32)
        _for_row_chunks(tm, clear)

    @pl.when(i < n_used)
    def _():
        @pl.when((i == 0) & (k == 0))
        def _():
            start_rows(0, 0, tm)

        @pl.when(k == 0)
        def _():
            wait_rows()

            def prep(rows):
                w = hbuf[rows, :]
                hi = lax.bitcast_convert_type(w & jnp.uint32(0xFFFF0000), F32)
                lo = lax.bitcast_convert_type(w << 16, F32)
                h_scr[rows, 0:half] = hi.astype(BF16)
                h_scr[rows, half:] = lo.astype(BF16)
                o_ref[rows, :] = jnp.zeros((EPI_ROWS, o_ref.shape[1]), F32)
            _for_row_chunks(tm, prep)

        @pl.when(i + 1 < n_used)
        def _():
            start_rows(i + 1, k * per_step, per_step)

        h = h_scr[...]
        g = jnp.dot(h, wg_ref[...], preferred_element_type=F32)
        u = jnp.dot(h, wu_ref[...], preferred_element_type=F32)
        _accumulate_dot(o_ref, (_silu(g) * u).astype(BF16), wd_ref)

        @pl.when(k == n_f - 1)
        def _():
            def fin(rows):
                o_ref[rows, :] = o_ref[rows, :] * rw_ref[rows, :]
            _for_row_chunks(tm, fin)


def _moe_call(blk_expert, n_used, row_tok, h_packed, row_w, w_g, w_u, w_down):
    tm, tf, d = MOE_TM, MOE_TF, D_MODEL
    n_f = MOE_N_F

    def row_blk(i, nu):
        return jnp.minimum(i, nu[0] - 1)

    def f_blk(i, k, nu):
        return jnp.where(i < nu[0], k, n_f - 1)

    blk = tm * V7X_LANES * 4 + 3 * d * tf * 2 + tm * d * 4
    scr = tm * (d // 2) * 4 + tm * d * 2
    grid_spec = pltpu.PrefetchScalarGridSpec(
        num_scalar_prefetch=3,
        grid=(MOE_N_BLK, n_f),
        in_specs=[
            pl.BlockSpec(memory_space=pl.ANY),
            pl.BlockSpec((tm, 1), lambda i, k, be, nu, rt: (row_blk(i, nu), 0)),
            pl.BlockSpec((None, d, tf),
                         lambda i, k, be, nu, rt: (be[row_blk(i, nu)], 0, f_blk(i, k, nu))),
            pl.BlockSpec((None, d, tf),
                         lambda i, k, be, nu, rt: (be[row_blk(i, nu)], 0, f_blk(i, k, nu))),
            pl.BlockSpec((None, tf, d),
                         lambda i, k, be, nu, rt: (be[row_blk(i, nu)], f_blk(i, k, nu), 0)),
        ],
        out_specs=pl.BlockSpec((tm, d), lambda i, k, be, nu, rt: (i, 0)),
        scratch_shapes=[pltpu.VMEM((tm, d // 2), jnp.uint32), pltpu.VMEM((tm, d), BF16),
                        pltpu.SemaphoreType.DMA((1,))],
    )
    return pl.pallas_call(
        _moe_kernel,
        out_shape=jax.ShapeDtypeStruct((MOE_N_ROWS, d), F32),
        grid_spec=grid_spec,
        compiler_params=pltpu.CompilerParams(
            dimension_semantics=("arbitrary", "arbitrary"),
            vmem_limit_bytes=_vmem_limit(blk, scr, 6 * tm * tf * 4 + tm * ACC_TN * 4),
            disable_bounds_checks=True),
        name="moe_ffn",
    )(blk_expert, n_used, row_tok, h_packed, row_w, w_g, w_u, w_down)


def _final_kernel(dest_ref, x_ref, y_hbm, mod_ref, lng_ref, lnb_ref, o_ref, ybuf, sem):
    i = pl.program_id(0)
    tm = x_ref.shape[0]
    slot = i % 2

    def start_tile(tile, dst_slot):
        def body(t, carry):
            for j in range(TOP_K):
                _gather_row_copy(y_hbm, dest_ref[(tile * tm + t) * TOP_K + j],
                                 ybuf.at[dst_slot, j], t, sem.at[dst_slot]).start()
            return carry
        lax.fori_loop(0, tm, body, 0, unroll=DMA_UNROLL)

    def wait_tile(dst_slot):
        def body(t, carry):
            for j in range(TOP_K):
                _gather_row_copy(y_hbm, 0, ybuf.at[dst_slot, j], t, sem.at[dst_slot]).wait()
            return carry
        lax.fori_loop(0, tm, body, 0, unroll=DMA_UNROLL)

    @pl.when(i == 0)
    def _():
        start_tile(0, 0)

    @pl.when(i + 1 < pl.num_programs(0))
    def _():
        start_tile(i + 1, 1 - slot)

    wait_tile(slot)

    def fin(rows):
        y = ybuf[slot, 0, rows, :] + ybuf[slot, 1, rows, :]
        z = ALPHA * x_ref[rows, :] + (1.0 + mod_ref[5:6, :]) * y
        o_ref[rows, :] = _layer_norm(z, lng_ref[...], lnb_ref[...])
    _for_row_chunks(tm, fin)


def _final_call(dest, x, y_rows, mod, lng, lnb):
    tm, d = FIN_TM, D_MODEL
    tiles_per_seq = SEQ // tm
    blk = 2 * tm * d * 4 + ADA_CHUNKS * d * 4 + 2 * d * 4
    grid_spec = pltpu.PrefetchScalarGridSpec(
        num_scalar_prefetch=1,
        grid=(N_TOK // tm,),
        in_specs=[
            pl.BlockSpec((tm, d), lambda i, de: (i, 0)),
            pl.BlockSpec(memory_space=pl.ANY),
            pl.BlockSpec((None, ADA_CHUNKS, d), lambda i, de: (i // tiles_per_seq, 0, 0)),
            pl.BlockSpec((1, d), lambda i, de: (0, 0)),
            pl.BlockSpec((1, d), lambda i, de: (0, 0)),
        ],
        out_specs=pl.BlockSpec((tm, d), lambda i, de: (i, 0)),
        scratch_shapes=[pltpu.VMEM((2, TOP_K, tm, d), F32), pltpu.SemaphoreType.DMA((2,))],
    )
    return pl.pallas_call(
        _final_kernel,
        out_shape=jax.ShapeDtypeStruct((N_TOK, d), F32),
        grid_spec=grid_spec,
        compiler_params=pltpu.CompilerParams(
            dimension_semantics=("arbitrary",),
            vmem_limit_bytes=_vmem_limit(blk, 2 * TOP_K * tm * d * 4, 8 * EPI_ROWS * d * 4),
            disable_bounds_checks=True),
        name="final_ln",
    )(dest, x, y_rows, mod, lng, lnb)


def _route(logits):
    top_logit, top_idx = lax.top_k(logits, TOP_K)
    top_w = jax.nn.softmax(top_logit, axis=-1)
    e_flat = top_idx.reshape(MOE_N_ASG).astype(jnp.int32)
    w_flat = top_w.reshape(MOE_N_ASG)
    tok_flat = jnp.arange(MOE_N_ASG, dtype=jnp.int32) // TOP_K
    onehot = (e_flat[:, None] == jnp.arange(N_EXPERTS, dtype=jnp.int32)[None, :]).astype(jnp.int32)
    csum = jnp.cumsum(onehot, axis=0)
    rank = jnp.sum((csum - onehot) * onehot, axis=1)
    counts = csum[-1]
    padded = (counts + MOE_TM - 1) // MOE_TM * MOE_TM
    padded_ends = jnp.cumsum(padded)
    padded_starts = padded_ends - padded
    dest = padded_starts[e_flat] + rank
    n_used = (padded_ends[-1:] // MOE_TM).astype(jnp.int32)
    blk_start = jnp.arange(MOE_N_BLK, dtype=jnp.int32) * MOE_TM
    blk_expert = jnp.minimum(jnp.searchsorted(padded_ends, blk_start, side='right'),
                             N_EXPERTS - 1).astype(jnp.int32)
    row_tok = jnp.zeros((MOE_N_ROWS,), jnp.int32).at[dest].set(tok_flat)
    row_w = jnp.zeros((MOE_N_ROWS,), F32).at[dest].set(w_flat)
    return blk_expert, n_used, dest, row_tok, row_w


def _rope_tables(positions):
    inv_freq = ROPE_THETA ** (-jnp.arange(0, ROT_DIM, 2, dtype=F32) / ROT_DIM)
    ang = positions.astype(F32)[..., None] * inv_freq
    cos = jnp.cos(ang).reshape(N_TOK, ROT_HALF)
    sin = jnp.sin(ang).reshape(N_TOK, ROT_HALF)
    rest = HEAD_DIM - ROT_DIM
    c = jnp.concatenate([cos, cos, jnp.ones((N_TOK, rest), F32)], axis=1)
    s_up = jnp.concatenate([jnp.zeros((N_TOK, ROT_HALF), F32), sin,
                            jnp.zeros((N_TOK, rest), F32)], axis=1)
    s_dn = jnp.concatenate([-sin, jnp.zeros((N_TOK, HEAD_DIM - ROT_HALF), F32)], axis=1)
    return c, s_up, s_dn


def kernel(x, c, positions, ada_w, ada_b, ln_g, ln_b, kv_ada_w, kv_ada_b, w_pool, pool_scale,
           w_kv, w_q, w_o, lam_q1, lam_k1, lam_q2, lam_k2, subln_g, ffn_w_gu, ffn_w_down,
           router_w, moe_w_gu, moe_w_down):
    d = D_MODEL
    xt = x.reshape(N_TOK, d)

    c8 = jnp.pad(c, ((0, V7X_SUBLANES - BATCH), (0, 0)))
    mods = _ada_call(c8, ada_w, ada_b)[:, :BATCH].reshape(DEPTH, BATCH, ADA_CHUNKS, d)
    mod_kv = _ada_call(c8, kv_ada_w[None], kv_ada_b[None])[0, :BATCH].reshape(BATCH, 2, d)

    x1 = _pool_call(xt, mods[0], w_pool[0].astype(BF16), pool_scale[0][None],
                    ln_g[0, 0][None], ln_b[0, 0][None])
    x2 = _ffn_call(x1, mods[0], ffn_w_gu[0].astype(BF16), ffn_w_down[0].astype(BF16),
                   ln_g[0, 1][None], ln_b[0, 1][None])

    rope_c, rope_up, rope_dn = _rope_tables(positions)
    w_qkv = jnp.concatenate([w_q[0].astype(BF16), w_kv.astype(BF16)], axis=1)
    qkv = _qkv_call(x2, mods[1], mod_kv, w_qkv, rope_c, rope_up, rope_dn)
    lam_vecs = jnp.stack([lam_q1[0], lam_k1[0], lam_q2[0], lam_k2[0]], axis=0)
    lambda_init = 0.8 - 0.6 * math.exp(-0.3 * 1)
    attn = _attn_call(qkv, lam_vecs, subln_g[0][None], lambda_init)
    w_router_t = router_w[0].T
    x3, h3, logits = _wo_call(attn, w_o[0].astype(BF16), x2, mods[1],
                              ln_g[1, 0][None], ln_b[1, 0][None], w_router_t)

    blk_expert, n_used, dest, row_tok, row_w = _route(logits[:, :N_EXPERTS])
    y_rows = _moe_call(blk_expert, n_used, row_tok, h3, row_w[:, None],
                       moe_w_gu[0, :, :, :D_FF_EXPERT].astype(BF16),
                       moe_w_gu[0, :, :, D_FF_EXPERT:].astype(BF16),
                       moe_w_down[0].astype(BF16))
    out = _final_call(dest, x3, y_rows, mods[1], ln_g[1, 1][None], ln_b[1, 1][None])
    return out.reshape(BATCH, SEQ, d)
```

```python
import functools
import math

import jax
import jax.numpy as jnp
from jax import lax
from jax.experimental import pallas as pl
from jax.experimental.pallas import tpu as pltpu

F32 = jnp.float32
BF16 = jnp.bfloat16

D_MODEL = 4096
BATCH = 2
SEQ = 8192
N_TOK = BATCH * SEQ
DEPTH = 2
POOL_WINDOWS = (2, 4, 8, 16)
POOL_GROUP_DIM = D_MODEL // len(POOL_WINDOWS)
POOL_HALO = 16
N_HEADS = 16
HEAD_DIM = 128
V_DIM = 2 * HEAD_DIM
ROT_DIM = HEAD_DIM // 4
ROT_HALF = ROT_DIM // 2
ROPE_THETA = 500000.0
D_FF = 11008
N_EXPERTS = 8
TOP_K = 2
D_FF_EXPERT = D_MODEL
LN_EPS = 1e-5
ALPHA = (2.0 * DEPTH) ** 0.25
ADA_CHUNKS = 6
LOG2E = math.log2(math.e)
Q_PRESCALE = HEAD_DIM ** -0.5 * LOG2E

V7X_VMEM_BYTES = 64 * 1024 * 1024
V7X_LANES = 128
V7X_SUBLANES = 8

ADA_TN = 1024
ADA_TK = 512
POOL_TS = 256
FFN_TM = 512
FFN_TF = 256
QKV_TM = 512
QKV_TN = 1024
ATT_TQ = 512
ATT_TK = 1024
WO_TM = 512
WO_TK = 512
MOE_TM = 512
MOE_TF = 512
MOE_N_F = D_FF_EXPERT // MOE_TF
FIN_TM = 256
CAST_TM = 512
ACC_TN = 1024
EPI_ROWS = 64
DMA_UNROLL = 8
ROUTER_PAD = V7X_LANES

MOE_N_ASG = N_TOK * TOP_K
MOE_N_BLK = MOE_N_ASG // MOE_TM + N_EXPERTS
MOE_N_ROWS = MOE_N_BLK * MOE_TM


def _vmem_limit(pipelined_bytes, resident_bytes=0, temp_bytes=0):
    need = 2 * pipelined_bytes + resident_bytes + temp_bytes + (2 << 20)
    assert need <= V7X_VMEM_BYTES - (2 << 20), need
    return int(need)


def _cparams(sem, vmem):
    return pltpu.CompilerParams(dimension_semantics=sem, vmem_limit_bytes=vmem)


def _layer_norm(z, g, b):
    mu = jnp.mean(z, axis=-1, keepdims=True)
    zc = z - mu
    var = jnp.mean(zc * zc, axis=-1, keepdims=True)
    return zc * lax.rsqrt(var + LN_EPS) * g + b


def _silu(x):
    return x / (1.0 + jnp.exp(-x))


def _for_row_chunks(n_rows, fn):
    def body(r, carry):
        fn(pl.ds(pl.multiple_of(r * EPI_ROWS, EPI_ROWS), EPI_ROWS))
        return carry
    lax.fori_loop(0, n_rows // EPI_ROWS, body, 0)


def _accumulate_dot(o_ref, a, w_ref):
    for j in range(o_ref.shape[1] // ACC_TN):
        cols = slice(j * ACC_TN, (j + 1) * ACC_TN)
        o_ref[:, cols] += jnp.dot(a, w_ref[:, cols], preferred_element_type=F32)


def _ada_kernel(c_ref, w_ref, b_ref, o_ref):
    acc = jnp.zeros(o_ref.shape, F32) + b_ref[...]
    for kk in range(c_ref.shape[1] // ADA_TK):
        ks = slice(kk * ADA_TK, (kk + 1) * ADA_TK)
        cond = _silu(c_ref[:, ks]).astype(BF16)
        acc = acc + jnp.dot(cond, w_ref[ks, :].astype(BF16), preferred_element_type=F32)
    o_ref[...] = acc


def _ada_call(c8, w, b):
    n_l, d, n = w.shape
    tn = ADA_TN
    blk = d * tn * 4 + 8 * d * 4 + 8 * tn * 4 + tn * 4
    return pl.pallas_call(
        _ada_kernel,
        out_shape=jax.ShapeDtypeStruct((n_l, 8, n), F32),
        grid=(n_l, n // tn),
        in_specs=[
            pl.BlockSpec((8, d), lambda l, j: (0, 0)),
            pl.BlockSpec((None, d, tn), lambda l, j: (l, 0, j)),
            pl.BlockSpec((None, 1, tn), lambda l, j: (l, 0, j)),
        ],
        out_specs=pl.BlockSpec((None, 8, tn), lambda l, j: (l, 0, j)),
        compiler_params=_cparams(("parallel", "parallel"),
                                 _vmem_limit(blk, temp_bytes=4 * ADA_TK * tn * 4)),
        name="ada_mod",
    )(c8, w, b.reshape(n_l, 1, n))


def _pool_kernel(x_ref, halo_ref, mod_ref, wp_ref, ps_ref, lng_ref, lnb_ref, o_ref):
    i = pl.program_id(1)
    ts = x_ref.shape[0]
    t1 = (i * ts + 1 + lax.broadcasted_iota(jnp.int32, (ts, 1), 0)).astype(F32)
    for g, w in enumerate(POOL_WINDOWS):
        cols = slice(g * POOL_GROUP_DIM, (g + 1) * POOL_GROUP_DIM)
        sh = mod_ref[0:1, cols]
        sc = mod_ref[1:2, cols]
        gate = mod_ref[2:3, cols]
        x = x_ref[:, cols]
        h = x * (1.0 + sc) + sh
        hh = jnp.where(i > 0, halo_ref[:, cols] * (1.0 + sc) + sh, 0.0)
        s = jnp.concatenate([hh, h], axis=0)
        span = 1
        while span < w:
            s = s + pltpu.roll(s, span, 0)
            span *= 2
        win = s[POOL_HALO:, :]
        pooled = win / jnp.minimum(t1, float(w)) - h
        mixed = jnp.dot(pooled.astype(BF16), wp_ref[g], preferred_element_type=F32)
        o_ref[:, cols] = ALPHA * x + (1.0 + gate) * (mixed * ps_ref[:, cols])

    def ln_rows(rows):
        o_ref[rows, :] = _layer_norm(o_ref[rows, :], lng_ref[...], lnb_ref[...])
    _for_row_chunks(ts, ln_rows)


def _pool_call(x, mod, wp, ps, lng, lnb):
    ts = POOL_TS
    n_s = SEQ // ts
    halo_per_tile = ts // POOL_HALO
    d = D_MODEL
    blk = ts * d * 4 * 2 + POOL_HALO * d * 4 + ADA_CHUNKS * d * 4 + 3 * d * 4
    return pl.pallas_call(
        _pool_kernel,
        out_shape=jax.ShapeDtypeStruct((N_TOK, d), F32),
        grid=(BATCH, n_s),
        in_specs=[
            pl.BlockSpec((ts, d), lambda b, i: (b * n_s + i, 0)),
            pl.BlockSpec((POOL_HALO, d),
                         lambda b, i: (jnp.maximum((b * n_s + i) * halo_per_tile - 1, 0), 0)),
            pl.BlockSpec((None, ADA_CHUNKS, d), lambda b, i: (b, 0, 0)),
            pl.BlockSpec(wp.shape, lambda b, i: (0, 0, 0), pipeline_mode=pl.Buffered(1)),
            pl.BlockSpec((1, d), lambda b, i: (0, 0)),
            pl.BlockSpec((1, d), lambda b, i: (0, 0)),
            pl.BlockSpec((1, d), lambda b, i: (0, 0)),
        ],
        out_specs=pl.BlockSpec((ts, d), lambda b, i: (b * n_s + i, 0)),
        compiler_params=_cparams(
            ("parallel", "parallel"),
            _vmem_limit(blk, wp.size * 2, 8 * (ts + POOL_HALO) * POOL_GROUP_DIM * 4)),
        name="pool_mixer",
    )(x, x, mod, wp, ps, lng, lnb)


def _ffn_kernel(x_ref, mod_ref, wg_ref, wu_ref, wd_ref, lng_ref, lnb_ref, o_ref, h_scr):
    k = pl.program_id(1)
    tm = x_ref.shape[0]

    @pl.when(k == 0)
    def _():
        def prep(rows):
            h_scr[rows, :] = (x_ref[rows, :] * (1.0 + mod_ref[4:5, :])
                              + mod_ref[3:4, :]).astype(BF16)
            o_ref[rows, :] = jnp.zeros((EPI_ROWS, o_ref.shape[1]), F32)
        _for_row_chunks(tm, prep)

    h = h_scr[...]
    g = jnp.dot(h, wg_ref[...], preferred_element_type=F32)
    u = jnp.dot(h, wu_ref[...], preferred_element_type=F32)
    _accumulate_dot(o_ref, (_silu(g) * u).astype(BF16), wd_ref)

    @pl.when(k == pl.num_programs(1) - 1)
    def _():
        def fin(rows):
            z = ALPHA * x_ref[rows, :] + (1.0 + mod_ref[5:6, :]) * o_ref[rows, :]
            o_ref[rows, :] = _layer_norm(z, lng_ref[...], lnb_ref[...])
        _for_row_chunks(tm, fin)


def _ffn_call(x, mod, w_gu, w_down, lng, lnb):
    tm, tf, d = FFN_TM, FFN_TF, D_MODEL
    n_f = D_FF // tf
    tiles_per_seq = SEQ // tm
    blk = tm * d * 4 * 2 + ADA_CHUNKS * d * 4 + 3 * d * tf * 2 + 2 * d * 4
    return pl.pallas_call(
        _ffn_kernel,
        out_shape=jax.ShapeDtypeStruct((N_TOK, d), F32),
        grid=(N_TOK // tm, n_f),
        in_specs=[
            pl.BlockSpec((tm, d), lambda i, k: (i, 0)),
            pl.BlockSpec((None, ADA_CHUNKS, d), lambda i, k: (i // tiles_per_seq, 0, 0)),
            pl.BlockSpec((d, tf), lambda i, k: (0, k)),
            pl.BlockSpec((d, tf), lambda i, k: (0, k + n_f)),
            pl.BlockSpec((tf, d), lambda i, k: (k, 0)),
            pl.BlockSpec((1, d), lambda i, k: (0, 0)),
            pl.BlockSpec((1, d), lambda i, k: (0, 0)),
        ],
        out_specs=pl.BlockSpec((tm, d), lambda i, k: (i, 0)),
        scratch_shapes=[pltpu.VMEM((tm, d), BF16)],
        compiler_params=_cparams(("parallel", "arbitrary"),
                                 _vmem_limit(blk, tm * d * 2, 6 * tm * tf * 4 + tm * ACC_TN * 4)),
        name="dense_ffn",
    )(x, mod, w_gu, w_gu, w_down, lng, lnb)


def _rope(r, c, s_up, s_dn):
    outs = []
    for j in range(r.shape[1] // HEAD_DIM):
        xc = r[:, j * HEAD_DIM:(j + 1) * HEAD_DIM]
        outs.append(xc * c + pltpu.roll(xc, ROT_HALF, 1) * s_up
                    + pltpu.roll(xc, HEAD_DIM - ROT_HALF, 1) * s_dn)
    return jnp.concatenate(outs, axis=1)


def _qkv_kernel(x_ref, modm_ref, modkv_ref, w_ref, c_ref, su_ref, sd_ref, o_ref,
                hq_scr, hkv_scr, *, n_q, n_k):
    n = pl.program_id(1)

    @pl.when(n == 0)
    def _():
        def prep(rows):
            x = x_ref[rows, :]
            hq_scr[rows, :] = (x * (1.0 + modm_ref[1:2, :]) + modm_ref[0:1, :]).astype(BF16)
            hkv_scr[rows, :] = (x * (1.0 + modkv_ref[1:2, :]) + modkv_ref[0:1, :]).astype(BF16)
        _for_row_chunks(x_ref.shape[0], prep)

    @pl.when(n < n_q)
    def _():
        r = jnp.dot(hq_scr[...], w_ref[...], preferred_element_type=F32)
        r = _rope(r, c_ref[...], su_ref[...], sd_ref[...]) * Q_PRESCALE
        o_ref[...] = r.astype(BF16)

    @pl.when((n >= n_q) & (n < n_q + n_k))
    def _():
        r = jnp.dot(hkv_scr[...], w_ref[...], preferred_element_type=F32)
        o_ref[...] = _rope(r, c_ref[...], su_ref[...], sd_ref[...]).astype(BF16)

    @pl.when(n >= n_q + n_k)
    def _():
        o_ref[...] = jnp.dot(hkv_scr[...], w_ref[...],
                             preferred_element_type=F32).astype(BF16)


def _qkv_call(x, modm, modkv, w_qkv, rope_c, rope_up, rope_dn):
    tm, tn, d = QKV_TM, QKV_TN, D_MODEL
    n_out = w_qkv.shape[1]
    tiles_per_seq = SEQ // tm
    blk = tm * d * 4 + (ADA_CHUNKS + 2) * d * 4 + d * tn * 2 + 3 * tm * HEAD_DIM * 4 + tm * tn * 2
    kern = functools.partial(_qkv_kernel, n_q=D_MODEL // tn, n_k=D_MODEL // tn)
    return pl.pallas_call(
        kern,
        out_shape=jax.ShapeDtypeStruct((N_TOK, n_out), BF16),
        grid=(N_TOK // tm, n_out // tn),
        in_specs=[
            pl.BlockSpec((tm, d), lambda i, n: (i, 0)),
            pl.BlockSpec((None, ADA_CHUNKS, d), lambda i, n: (i // tiles_per_seq, 0, 0)),
            pl.BlockSpec((None, 2, d), lambda i, n: (i // tiles_per_seq, 0, 0)),
            pl.BlockSpec((d, tn), lambda i, n: (0, n)),
            pl.BlockSpec((tm, HEAD_DIM), lambda i, n: (i, 0)),
            pl.BlockSpec((tm, HEAD_DIM), lambda i, n: (i, 0)),
            pl.BlockSpec((tm, HEAD_DIM), lambda i, n: (i, 0)),
        ],
        out_specs=pl.BlockSpec((tm, tn), lambda i, n: (i, n)),
        scratch_shapes=[pltpu.VMEM((tm, d), BF16), pltpu.VMEM((tm, d), BF16)],
        compiler_params=_cparams(("parallel", "arbitrary"),
                                 _vmem_limit(blk, 2 * tm * d * 2, 4 * tm * tn * 4)),
        name="qkv_proj",
    )(x, modm, modkv, w_qkv, rope_c, rope_up, rope_dn)


def _attn_kernel(q_ref, k_ref, v_ref, lam_ref, g_ref, o_ref, m_scr, l_scr, acc_scr,
                 *, lambda_init):
    i = pl.program_id(2)
    tq = q_ref.shape[0]
    tk = ATT_TK
    m_scr[...] = jnp.full(m_scr.shape, -jnp.inf, F32)
    l_scr[...] = jnp.zeros(l_scr.shape, F32)
    acc_scr[...] = jnp.zeros(acc_scr.shape, F32)

    def chunk(start, width, masked):
        kc = k_ref[pl.ds(start, width), :]
        vc = v_ref[pl.ds(start, width), :]
        for c in range(2):
            qc = q_ref[:, c * HEAD_DIM:(c + 1) * HEAD_DIM]
            s = lax.dot_general(qc, kc[:, c * HEAD_DIM:(c + 1) * HEAD_DIM],
                                (((1,), (1,)), ((), ())), preferred_element_type=F32)
            if masked:
                row = lax.broadcasted_iota(jnp.int32, (tq, width), 0)
                col = lax.broadcasted_iota(jnp.int32, (tq, width), 1)
                s = jnp.where(col <= row, s, -jnp.inf)
            lanes = [s[:, t * V7X_LANES:(t + 1) * V7X_LANES] for t in range(width // V7X_LANES)]
            m_prev = m_scr[c]
            m_new = jnp.maximum(m_prev, jnp.max(functools.reduce(jnp.maximum, lanes),
                                                axis=1, keepdims=True))
            a = jnp.exp2(m_prev - m_new)
            ps = [jnp.exp2(x - m_new) for x in lanes]
            l_scr[c] = a * l_scr[c] + functools.reduce(jnp.add, ps)
            p = jnp.concatenate([x.astype(BF16) for x in ps], axis=1)
            acc_scr[c] = (jnp.concatenate([a, a], axis=1) * acc_scr[c]
                          + jnp.dot(p, vc, preferred_element_type=F32))
            m_scr[c] = m_new

    per_wide = tk // tq
    n_wide = i // per_wide

    def body(j, carry):
        chunk(pl.multiple_of(2 * j * tk, tk), tk, False)
        chunk(pl.multiple_of((2 * j + 1) * tk, tk), tk, False)
        return carry

    lax.fori_loop(0, n_wide // 2, body, 0)

    @pl.when(n_wide % 2 == 1)
    def _():
        chunk(pl.multiple_of((n_wide - 1) * tk, tk), tk, False)

    for r in range(per_wide - 1):
        @pl.when(i - n_wide * per_wide > r)
        def _():
            chunk(pl.multiple_of((n_wide * per_wide + r) * tq, tq), tq, False)
    chunk(pl.multiple_of(i * tq, tq), tq, True)

    lam = (jnp.exp(jnp.sum(lam_ref[0:1, :] * lam_ref[1:2, :], axis=1, keepdims=True))
           - jnp.exp(jnp.sum(lam_ref[2:3, :] * lam_ref[3:4, :], axis=1, keepdims=True))
           + lambda_init)
    l0 = jnp.sum(l_scr[0], axis=1, keepdims=True)
    l1 = jnp.sum(l_scr[1], axis=1, keepdims=True)
    o = acc_scr[0] / l0 - lam * (acc_scr[1] / l1)
    ms = jnp.mean(o * o, axis=-1, keepdims=True)
    o_ref[...] = (o * lax.rsqrt(ms + LN_EPS) * g_ref[...] * (1.0 - lambda_init)).astype(BF16)


def _attn_call(qkv, lam_vecs, subln_g, lambda_init):
    assert ATT_TK % ATT_TQ == 0
    tq = ATT_TQ
    n_q = SEQ // tq
    kern = functools.partial(_attn_kernel, lambda_init=lambda_init)
    blk = 2 * tq * V_DIM * 2 + 2 * SEQ * V_DIM * 2 + 4 * HEAD_DIM * 4 + V_DIM * 4
    scr = 2 * tq * V_DIM * 4 + 4 * tq * V7X_LANES * 4
    return pl.pallas_call(
        kern,
        out_shape=jax.ShapeDtypeStruct((N_TOK, D_MODEL), BF16),
        grid=(BATCH, N_HEADS, n_q),
        in_specs=[
            pl.BlockSpec((tq, V_DIM), lambda b, h, i: (b * n_q + i, h)),
            pl.BlockSpec((SEQ, V_DIM), lambda b, h, i: (b, N_HEADS + h)),
            pl.BlockSpec((SEQ, V_DIM), lambda b, h, i: (b, 2 * N_HEADS + h)),
            pl.BlockSpec((4, HEAD_DIM), lambda b, h, i: (0, 0)),
            pl.BlockSpec((1, V_DIM), lambda b, h, i: (0, 0)),
        ],
        out_specs=pl.BlockSpec((tq, V_DIM), lambda b, h, i: (b * n_q + i, h)),
        scratch_shapes=[pltpu.VMEM((2, tq, V7X_LANES), F32), pltpu.VMEM((2, tq, V7X_LANES), F32),
                        pltpu.VMEM((2, tq, V_DIM), F32)],
        compiler_params=_cparams(("parallel", "parallel", "arbitrary"),
                                 _vmem_limit(blk, scr, 8 * tq * ATT_TK * 4)),
        name="diff_attn",
    )(qkv, qkv, qkv, lam_vecs, subln_g)


def _wo_kernel(a_ref, w_ref, x_ref, mod_ref, lng_ref, lnb_ref, wr_ref,
               o_ref, h_ref, lg_ref):
    k = pl.program_id(1)
    tm = x_ref.shape[0]

    @pl.when(k == 0)
    def _():
        def prep(rows):
            o_ref[rows, :] = jnp.zeros((EPI_ROWS, o_ref.shape[1]), F32)
        _for_row_chunks(tm, prep)

    _accumulate_dot(o_ref, a_ref[...], w_ref)

    @pl.when(k == pl.num_programs(1) - 1)
    def _():
        def fin(rows):
            z = ALPHA * x_ref[rows, :] + (1.0 + mod_ref[2:3, :]) * o_ref[rows, :]
            xn = _layer_norm(z, lng_ref[...], lnb_ref[...])
            o_ref[rows, :] = xn
            h = xn * (1.0 + mod_ref[4:5, :]) + mod_ref[3:4, :]
            bits = lax.bitcast_convert_type(h.astype(BF16).astype(F32), jnp.uint32)
            half = bits.shape[1] // 2
            h_ref[rows, :] = bits[:, :half] | (bits[:, half:] >> 16)
            lane = lax.broadcasted_iota(jnp.int32, (EPI_ROWS, ROUTER_PAD), 1)
            lg = jnp.zeros((EPI_ROWS, ROUTER_PAD), F32)
            for e in range(N_EXPERTS):
                val = jnp.sum(h * wr_ref[e:e + 1, :], axis=1, keepdims=True)
                lg = jnp.where(lane == e, val, lg)
            lg_ref[rows, :] = lg
        _for_row_chunks(tm, fin)


def _wo_call(attn, w_o, x, mod, lng, lnb, w_router_t):
    tm, tk, d = WO_TM, WO_TK, D_MODEL
    tiles_per_seq = SEQ // tm
    blk = (tm * tk * 2 + tk * d * 2 + tm * d * 4 * 2 + tm * d * 2 + ADA_CHUNKS * d * 4
           + 2 * d * 4 + tm * ROUTER_PAD * 4)
    return pl.pallas_call(
        _wo_kernel,
        out_shape=(jax.ShapeDtypeStruct((N_TOK, d), F32),
                   jax.ShapeDtypeStruct((N_TOK, d // 2), jnp.uint32),
                   jax.ShapeDtypeStruct((N_TOK, ROUTER_PAD), F32)),
        grid=(N_TOK // tm, d // tk),
        in_specs=[
            pl.BlockSpec((tm, tk), lambda i, k: (i, k)),
            pl.BlockSpec((tk, d), lambda i, k: (k, 0)),
            pl.BlockSpec((tm, d), lambda i, k: (i, 0)),
            pl.BlockSpec((None, ADA_CHUNKS, d), lambda i, k: (i // tiles_per_seq, 0, 0)),
            pl.BlockSpec((1, d), lambda i, k: (0, 0)),
            pl.BlockSpec((1, d), lambda i, k: (0, 0)),
            pl.BlockSpec((N_EXPERTS, d), lambda i, k: (0, 0)),
        ],
        out_specs=(pl.BlockSpec((tm, d), lambda i, k: (i, 0)),
                   pl.BlockSpec((tm, d // 2), lambda i, k: (i, 0)),
                   pl.BlockSpec((tm, ROUTER_PAD), lambda i, k: (i, 0))),
        compiler_params=_cparams(("parallel", "arbitrary"),
                                 _vmem_limit(blk, 2 * N_EXPERTS * d * 4,
                                             tm * ACC_TN * 4 + 6 * EPI_ROWS * d * 4)),
        name="attn_out_proj",
    )(attn, w_o, x, mod, lng, lnb, w_router_t)


def _gather_row_copy(src_hbm, row, dst_buf, dst_row, sem):
    return pltpu.make_async_copy(src_hbm.at[pl.ds(row, 1), :], dst_buf.at[pl.ds(dst_row, 1), :], sem)


def _moe_kernel(be_ref, nu_ref, rt_ref, h_hbm, wg_ref, wu_ref, wd_ref, o_ref,
                hbuf, h_scr, sem):
    i = pl.program_id(0)
    k = pl.program_id(1)
    n_f = pl.num_programs(1)
    tm = h_scr.shape[0]
    half = h_scr.shape[1] // 2
    per_step = tm // MOE_N_F
    n_used = nu_ref[0]

    def start_rows(blk, first, count):
        def body(r, carry):
            row = first + r
            _gather_row_copy(h_hbm, rt_ref[blk * tm + row], hbuf, row, sem.at[0]).start()
            return carry
        lax.fori_loop(0, count, body, 0, unroll=DMA_UNROLL)

    def wait_rows():
        def body(r, carry):
            _gather_row_copy(h_hbm, 0, hbuf, r, sem.at[0]).wait()
            return carry
        lax.fori_loop(0, tm, body, 0, unroll=DMA_UNROLL)

    @pl.when((i >= n_used) & (k == 0))
    def _():
        def clear(rows):
            o_ref[rows, :] = jnp.zeros((EPI_ROWS, o_ref.shape[1]), F32)
        _for_row_chunks(tm, clear)

    @pl.when(i < n_used)
    def _():
        @pl.when((i == 0) & (k == 0))
        def _():
            start_rows(0, 0, tm)

        @pl.when(k == 0)
        def _():
            wait_rows()

            def prep(rows):
                w = hbuf[rows, :]
                hi = lax.bitcast_convert_type(w & jnp.uint32(0xFFFF0000), F32)
                lo = lax.bitcast_convert_type(w << 16, F32)
                h_scr[rows, 0:half] = hi.astype(BF16)
                h_scr[rows, half:] = lo.astype(BF16)
                o_ref[rows, :] = jnp.zeros((EPI_ROWS, o_ref.shape[1]), F32)
            _for_row_chunks(tm, prep)

        nxt = jnp.minimum(i + 1, n_used - 1)
        for r in range(per_step):
            row = k * per_step + r
            _gather_row_copy(h_hbm, rt_ref[nxt * tm + row], hbuf, row, sem.at[0]).start()

        h = h_scr[...]
        g = jnp.dot(h, wg_ref[...], preferred_element_type=F32)
        u = jnp.dot(h, wu_ref[...], preferred_element_type=F32)
        _accumulate_dot(o_ref, (_silu(g) * u).astype(BF16), wd_ref)

        @pl.when((i == n_used - 1) & (k == n_f - 1))
        def _():
            wait_rows()


def _moe_call(blk_expert, n_used, row_tok, h_packed, w_g, w_u, w_down):
    tm, tf, d = MOE_TM, MOE_TF, D_MODEL
    n_f = MOE_N_F

    def row_blk(i, nu):
        return jnp.minimum(i, nu[0] - 1)

    def f_blk(i, k, nu):
        return jnp.where(i < nu[0], k, n_f - 1)

    blk = 3 * d * tf * 2 + tm * d * 4
    scr = tm * (d // 2) * 4 + tm * d * 2
    grid_spec = pltpu.PrefetchScalarGridSpec(
        num_scalar_prefetch=3,
        grid=(MOE_N_BLK, n_f),
        in_specs=[
            pl.BlockSpec(memory_space=pl.ANY),
            pl.BlockSpec((None, d, tf),
                         lambda i, k, be, nu, rt: (be[row_blk(i, nu)], 0, f_blk(i, k, nu))),
            pl.BlockSpec((None, d, tf),
                         lambda i, k, be, nu, rt: (be[row_blk(i, nu)], 0, f_blk(i, k, nu))),
            pl.BlockSpec((None, tf, d),
                         lambda i, k, be, nu, rt: (be[row_blk(i, nu)], f_blk(i, k, nu), 0)),
        ],
        out_specs=pl.BlockSpec((tm, d), lambda i, k, be, nu, rt: (i, 0)),
        scratch_shapes=[pltpu.VMEM((tm, d // 2), jnp.uint32), pltpu.VMEM((tm, d), BF16),
                        pltpu.SemaphoreType.DMA((1,))],
    )
    return pl.pallas_call(
        _moe_kernel,
        out_shape=jax.ShapeDtypeStruct((MOE_N_ROWS, d), F32),
        grid_spec=grid_spec,
        compiler_params=pltpu.CompilerParams(
            dimension_semantics=("arbitrary", "arbitrary"),
            vmem_limit_bytes=_vmem_limit(blk, scr, 6 * tm * tf * 4 + tm * ACC_TN * 4),
            disable_bounds_checks=True),
        name="moe_ffn",
    )(blk_expert, n_used, row_tok, h_packed, w_g, w_u, w_down)


def _final_kernel(dest_ref, x_ref, y_hbm, tw_ref, mod_ref, lng_ref, lnb_ref, o_ref, ybuf, sem):
    i = pl.program_id(0)
    tm = x_ref.shape[0]
    slot = i % 2

    def start_tile(tile, dst_slot):
        def body(t, carry):
            for j in range(TOP_K):
                _gather_row_copy(y_hbm, dest_ref[(tile * tm + t) * TOP_K + j],
                                 ybuf.at[dst_slot, j], t, sem.at[dst_slot]).start()
            return carry
        lax.fori_loop(0, tm, body, 0, unroll=DMA_UNROLL)

    def wait_tile(dst_slot):
        def body(t, carry):
            for j in range(TOP_K):
                _gather_row_copy(y_hbm, 0, ybuf.at[dst_slot, j], t, sem.at[dst_slot]).wait()
            return carry
        lax.fori_loop(0, tm, body, 0, unroll=DMA_UNROLL)

    @pl.when(i == 0)
    def _():
        start_tile(0, 0)

    @pl.when(i + 1 < pl.num_programs(0))
    def _():
        start_tile(i + 1, 1 - slot)

    wait_tile(slot)

    def fin(rows):
        y = (tw_ref[rows, 0:1] * ybuf[slot, 0, rows, :]
             + tw_ref[rows, 1:2] * ybuf[slot, 1, rows, :])
        z = ALPHA * x_ref[rows, :] + (1.0 + mod_ref[5:6, :]) * y
        o_ref[rows, :] = _layer_norm(z, lng_ref[...], lnb_ref[...])
    _for_row_chunks(tm, fin)


def _final_call(dest, x, y_rows, top_w, mod, lng, lnb):
    tm, d = FIN_TM, D_MODEL
    tiles_per_seq = SEQ // tm
    blk = 2 * tm * d * 4 + tm * V7X_LANES * 4 + ADA_CHUNKS * d * 4 + 2 * d * 4
    grid_spec = pltpu.PrefetchScalarGridSpec(
        num_scalar_prefetch=1,
        grid=(N_TOK // tm,),
        in_specs=[
            pl.BlockSpec((tm, d), lambda i, de: (i, 0)),
            pl.BlockSpec(memory_space=pl.ANY),
            pl.BlockSpec((tm, TOP_K), lambda i, de: (i, 0)),
            pl.BlockSpec((None, ADA_CHUNKS, d), lambda i, de: (i // tiles_per_seq, 0, 0)),
            pl.BlockSpec((1, d), lambda i, de: (0, 0)),
            pl.BlockSpec((1, d), lambda i, de: (0, 0)),
        ],
        out_specs=pl.BlockSpec((tm, d), lambda i, de: (i, 0)),
        scratch_shapes=[pltpu.VMEM((2, TOP_K, tm, d), F32), pltpu.SemaphoreType.DMA((2,))],
    )
    return pl.pallas_call(
        _final_kernel,
        out_shape=jax.ShapeDtypeStruct((N_TOK, d), F32),
        grid_spec=grid_spec,
        compiler_params=pltpu.CompilerParams(
            dimension_semantics=("arbitrary",),
            vmem_limit_bytes=_vmem_limit(blk, 2 * TOP_K * tm * d * 4, 8 * EPI_ROWS * d * 4),
            disable_bounds_checks=True),
        name="final_ln",
    )(dest, x, y_rows, top_w, mod, lng, lnb)


def _cast_split_kernel(w_ref, g_ref, u_ref):
    n = g_ref.shape[1]

    def cast(rows):
        g_ref[rows, :] = w_ref[rows, 0:n].astype(BF16)
        u_ref[rows, :] = w_ref[rows, n:].astype(BF16)
    _for_row_chunks(w_ref.shape[0], cast)


def _cast_split_call(w):
    r, n2 = w.shape
    n = n2 // 2
    tm = CAST_TM
    blk = tm * n2 * 4 + 2 * tm * n * 2
    return pl.pallas_call(
        _cast_split_kernel,
        out_shape=(jax.ShapeDtypeStruct((r, n), BF16), jax.ShapeDtypeStruct((r, n), BF16)),
        grid=(r // tm,),
        in_specs=[pl.BlockSpec((tm, n2), lambda i: (i, 0))],
        out_specs=(pl.BlockSpec((tm, n), lambda i: (i, 0)),
                   pl.BlockSpec((tm, n), lambda i: (i, 0))),
        compiler_params=_cparams(("parallel",), _vmem_limit(blk, 0, 4 * EPI_ROWS * n2 * 4)),
        name="cast_gate_up",
    )(w)


def _route(logits):
    top_logit, top_idx = lax.top_k(logits, TOP_K)
    top_w = jax.nn.softmax(top_logit, axis=-1)
    e_flat = top_idx.reshape(MOE_N_ASG).astype(jnp.int32)
    tok_flat = jnp.arange(MOE_N_ASG, dtype=jnp.int32) // TOP_K
    onehot = (e_flat[:, None] == jnp.arange(N_EXPERTS, dtype=jnp.int32)[None, :]).astype(jnp.int32)
    csum = jnp.cumsum(onehot, axis=0)
    rank = jnp.sum((csum - onehot) * onehot, axis=1)
    counts = csum[-1]
    padded = (counts + MOE_TM - 1) // MOE_TM * MOE_TM
    padded_ends = jnp.cumsum(padded)
    padded_starts = padded_ends - padded
    dest = padded_starts[e_flat] + rank
    n_used = (padded_ends[-1:] // MOE_TM).astype(jnp.int32)
    blk_start = jnp.arange(MOE_N_BLK, dtype=jnp.int32) * MOE_TM
    blk_expert = jnp.minimum(jnp.searchsorted(padded_ends, blk_start, side='right'),
                             N_EXPERTS - 1).astype(jnp.int32)
    row_tok = jnp.zeros((MOE_N_ROWS,), jnp.int32).at[dest].set(tok_flat)
    return blk_expert, n_used, dest, row_tok, top_w


def _rope_tables(positions):
    inv_freq = ROPE_THETA ** (-jnp.arange(0, ROT_DIM, 2, dtype=F32) / ROT_DIM)
    ang = positions.astype(F32)[..., None] * inv_freq
    cos = jnp.cos(ang).reshape(N_TOK, ROT_HALF)
    sin = jnp.sin(ang).reshape(N_TOK, ROT_HALF)
    rest = HEAD_DIM - ROT_DIM
    c = jnp.concatenate([cos, cos, jnp.ones((N_TOK, rest), F32)], axis=1)
    s_up = jnp.concatenate([jnp.zeros((N_TOK, ROT_HALF), F32), sin,
                            jnp.zeros((N_TOK, rest), F32)], axis=1)
    s_dn = jnp.concatenate([-sin, jnp.zeros((N_TOK, HEAD_DIM - ROT_HALF), F32)], axis=1)
    return c, s_up, s_dn


def kernel(x, c, positions, ada_w, ada_b, ln_g, ln_b, kv_ada_w, kv_ada_b, w_pool, pool_scale,
           w_kv, w_q, w_o, lam_q1, lam_k1, lam_q2, lam_k2, subln_g, ffn_w_gu, ffn_w_down,
           router_w, moe_w_gu, moe_w_down):
    d = D_MODEL
    xt = x.reshape(N_TOK, d)

    c8 = jnp.pad(c, ((0, V7X_SUBLANES - BATCH), (0, 0)))
    mods = _ada_call(c8, ada_w, ada_b)[:, :BATCH].reshape(DEPTH, BATCH, ADA_CHUNKS, d)
    mod_kv = _ada_call(c8, kv_ada_w[None], kv_ada_b[None])[0, :BATCH].reshape(BATCH, 2, d)

    x1 = _pool_call(xt, mods[0], w_pool[0].astype(BF16), pool_scale[0][None],
                    ln_g[0, 0][None], ln_b[0, 0][None])
    x2 = _ffn_call(x1, mods[0], ffn_w_gu[0].astype(BF16), ffn_w_down[0].astype(BF16),
                   ln_g[0, 1][None], ln_b[0, 1][None])

    rope_c, rope_up, rope_dn = _rope_tables(positions)
    w_qkv = jnp.concatenate([w_q[0].astype(BF16), w_kv.astype(BF16)], axis=1)
    qkv = _qkv_call(x2, mods[1], mod_kv, w_qkv, rope_c, rope_up, rope_dn)
    lam_vecs = jnp.stack([lam_q1[0], lam_k1[0], lam_q2[0], lam_k2[0]], axis=0)
    lambda_init = 0.8 - 0.6 * math.exp(-0.3 * 1)
    attn = _attn_call(qkv, lam_vecs, subln_g[0][None], lambda_init)
    w_router_t = router_w[0].T
    x3, h3, logits = _wo_call(attn, w_o[0].astype(BF16), x2, mods[1],
                              ln_g[1, 0][None], ln_b[1, 0][None], w_router_t)

    blk_expert, n_used, dest, row_tok, top_w = _route(logits[:, :N_EXPERTS])
    w_g, w_u = _cast_split_call(moe_w_gu[0].reshape(N_EXPERTS * d, 2 * D_FF_EXPERT))
    y_rows = _moe_call(blk_expert, n_used, row_tok, h3,
                       w_g.reshape(N_EXPERTS, d, D_FF_EXPERT),
                       w_u.reshape(N_EXPERTS, d, D_FF_EXPERT),
                       moe_w_down[0].astype(BF16))
    out = _final_call(dest, x3, y_rows, top_w, mods[1], ln_g[1, 1][None], ln_b[1, 1][None])
    return out.reshape(BATCH, SEQ, d)
```

```python
import functools
import math

import jax
import jax.numpy as jnp
from jax import lax
from jax.experimental import pallas as pl
from jax.experimental.pallas import tpu as pltpu

F32 = jnp.float32
BF16 = jnp.bfloat16

D_MODEL = 4096
BATCH = 2
SEQ = 8192
N_TOK = BATCH * SEQ
DEPTH = 2
POOL_WINDOWS = (2, 4, 8, 16)
POOL_GROUP_DIM = D_MODEL // len(POOL_WINDOWS)
POOL_HALO = 16
N_HEADS = 16
HEAD_DIM = 128
V_DIM = 2 * HEAD_DIM
ROT_DIM = HEAD_DIM // 4
ROT_HALF = ROT_DIM // 2
ROPE_THETA = 500000.0
D_FF = 11008
N_EXPERTS = 8
TOP_K = 2
D_FF_EXPERT = D_MODEL
LN_EPS = 1e-5
ALPHA = (2.0 * DEPTH) ** 0.25
ADA_CHUNKS = 6
LOG2E = math.log2(math.e)
Q_PRESCALE = HEAD_DIM ** -0.5 * LOG2E

V7X_VMEM_BYTES = 64 * 1024 * 1024
V7X_LANES = 128
V7X_SUBLANES = 8

ADA_TN = 1024
ADA_TK = 512
POOL_TS = 256
FFN_TM = 512
FFN_TF = 256
QKV_TM = 512
QKV_TN = 1024
ATT_TQ = 512
ATT_TK = 1024
WO_TM = 512
WO_TK = 512
MOE_TM = 512
MOE_TF = 512
MOE_N_F = D_FF_EXPERT // MOE_TF
FIN_TM = 256
CAST_TM = 512
ACC_TN = 1024
EPI_ROWS = 64
DMA_UNROLL = 8
ROUTER_PAD = V7X_LANES

MOE_N_ASG = N_TOK * TOP_K
MOE_N_BLK = MOE_N_ASG // MOE_TM + N_EXPERTS
MOE_N_ROWS = MOE_N_BLK * MOE_TM


def _vmem_limit(pipelined_bytes, resident_bytes=0, temp_bytes=0):
    need = 2 * pipelined_bytes + resident_bytes + temp_bytes + (2 << 20)
    assert need <= V7X_VMEM_BYTES - (2 << 20), need
    return int(need)


def _cparams(sem, vmem):
    return pltpu.CompilerParams(dimension_semantics=sem, vmem_limit_bytes=vmem)


def _layer_norm(z, g, b):
    mu = jnp.mean(z, axis=-1, keepdims=True)
    zc = z - mu
    var = jnp.mean(zc * zc, axis=-1, keepdims=True)
    return zc * lax.rsqrt(var + LN_EPS) * g + b


def _silu(x):
    return x / (1.0 + jnp.exp(-x))


def _for_row_chunks(n_rows, fn):
    def body(r, carry):
        fn(pl.ds(pl.multiple_of(r * EPI_ROWS, EPI_ROWS), EPI_ROWS))
        return carry
    lax.fori_loop(0, n_rows // EPI_ROWS, body, 0)


def _accumulate_dot(o_ref, a, w_ref):
    for j in range(o_ref.shape[1] // ACC_TN):
        cols = slice(j * ACC_TN, (j + 1) * ACC_TN)
        o_ref[:, cols] += jnp.dot(a, w_ref[:, cols], preferred_element_type=F32)


def _ada_kernel(c_ref, w_ref, b_ref, o_ref):
    acc = jnp.zeros(o_ref.shape, F32) + b_ref[...]
    for kk in range(c_ref.shape[1] // ADA_TK):
        ks = slice(kk * ADA_TK, (kk + 1) * ADA_TK)
        cond = _silu(c_ref[:, ks]).astype(BF16)
        acc = acc + jnp.dot(cond, w_ref[ks, :].astype(BF16), preferred_element_type=F32)
    o_ref[...] = acc


def _ada_call(c8, w, b):
    n_l, d, n = w.shape
    tn = ADA_TN
    blk = d * tn * 4 + 8 * d * 4 + 8 * tn * 4 + tn * 4
    return pl.pallas_call(
        _ada_kernel,
        out_shape=jax.ShapeDtypeStruct((n_l, 8, n), F32),
        grid=(n_l, n // tn),
        in_specs=[
            pl.BlockSpec((8, d), lambda l, j: (0, 0)),
            pl.BlockSpec((None, d, tn), lambda l, j: (l, 0, j)),
            pl.BlockSpec((None, 1, tn), lambda l, j: (l, 0, j)),
        ],
        out_specs=pl.BlockSpec((None, 8, tn), lambda l, j: (l, 0, j)),
        compiler_params=_cparams(("parallel", "parallel"),
                                 _vmem_limit(blk, temp_bytes=4 * ADA_TK * tn * 4)),
        name="ada_mod",
    )(c8, w, b.reshape(n_l, 1, n))


def _pool_kernel(x_ref, halo_ref, mod_ref, wp_ref, ps_ref, lng_ref, lnb_ref, o_ref):
    i = pl.program_id(1)
    ts = x_ref.shape[0]
    t1 = (i * ts + 1 + lax.broadcasted_iota(jnp.int32, (ts, 1), 0)).astype(F32)
    for g, w in enumerate(POOL_WINDOWS):
        cols = slice(g * POOL_GROUP_DIM, (g + 1) * POOL_GROUP_DIM)
        sh = mod_ref[0:1, cols]
        sc = mod_ref[1:2, cols]
        gate = mod_ref[2:3, cols]
        x = x_ref[:, cols]
        h = x * (1.0 + sc) + sh
        hh = jnp.where(i > 0, halo_ref[:, cols] * (1.0 + sc) + sh, 0.0)
        s = jnp.concatenate([hh, h], axis=0)
        span = 1
        while span < w:
            s = s + pltpu.roll(s, span, 0)
            span *= 2
        win = s[POOL_HALO:, :]
        pooled = win / jnp.minimum(t1, float(w)) - h
        mixed = jnp.dot(pooled.astype(BF16), wp_ref[g], preferred_element_type=F32)
        o_ref[:, cols] = ALPHA * x + (1.0 + gate) * (mixed * ps_ref[:, cols])

    def ln_rows(rows):
        o_ref[rows, :] = _layer_norm(o_ref[rows, :], lng_ref[...], lnb_ref[...])
    _for_row_chunks(ts, ln_rows)


def _pool_call(x, mod, wp, ps, lng, lnb):
    ts = POOL_TS
    n_s = SEQ // ts
    halo_per_tile = ts // POOL_HALO
    d = D_MODEL
    blk = ts * d * 4 * 2 + POOL_HALO * d * 4 + ADA_CHUNKS * d * 4 + 3 * d * 4
    return pl.pallas_call(
        _pool_kernel,
        out_shape=jax.ShapeDtypeStruct((N_TOK, d), F32),
        grid=(BATCH, n_s),
        in_specs=[
            pl.BlockSpec((ts, d), lambda b, i: (b * n_s + i, 0)),
            pl.BlockSpec((POOL_HALO, d),
                         lambda b, i: (jnp.maximum((b * n_s + i) * halo_per_tile - 1, 0), 0)),
            pl.BlockSpec((None, ADA_CHUNKS, d), lambda b, i: (b, 0, 0)),
            pl.BlockSpec(wp.shape, lambda b, i: (0, 0, 0), pipeline_mode=pl.Buffered(1)),
            pl.BlockSpec((1, d), lambda b, i: (0, 0)),
            pl.BlockSpec((1, d), lambda b, i: (0, 0)),
            pl.BlockSpec((1, d), lambda b, i: (0, 0)),
        ],
        out_specs=pl.BlockSpec((ts, d), lambda b, i: (b * n_s + i, 0)),
        compiler_params=_cparams(
            ("parallel", "parallel"),
            _vmem_limit(blk, wp.size * 2, 8 * (ts + POOL_HALO) * POOL_GROUP_DIM * 4)),
        name="pool_mixer",
    )(x, x, mod, wp, ps, lng, lnb)


def _ffn_kernel(x_ref, mod_ref, wg_ref, wu_ref, wd_ref, lng_ref, lnb_ref, o_ref, h_scr):
    k = pl.program_id(1)
    tm = x_ref.shape[0]

    @pl.when(k == 0)
    def _():
        def prep(rows):
            h_scr[rows, :] = (x_ref[rows, :] * (1.0 + mod_ref[4:5, :])
                              + mod_ref[3:4, :]).astype(BF16)
            o_ref[rows, :] = jnp.zeros((EPI_ROWS, o_ref.shape[1]), F32)
        _for_row_chunks(tm, prep)

    h = h_scr[...]
    g = jnp.dot(h, wg_ref[...], preferred_element_type=F32)
    u = jnp.dot(h, wu_ref[...], preferred_element_type=F32)
    _accumulate_dot(o_ref, (_silu(g) * u).astype(BF16), wd_ref)

    @pl.when(k == pl.num_programs(1) - 1)
    def _():
        def fin(rows):
            z = ALPHA * x_ref[rows, :] + (1.0 + mod_ref[5:6, :]) * o_ref[rows, :]
            o_ref[rows, :] = _layer_norm(z, lng_ref[...], lnb_ref[...])
        _for_row_chunks(tm, fin)


def _ffn_call(x, mod, w_gu, w_down, lng, lnb):
    tm, tf, d = FFN_TM, FFN_TF, D_MODEL
    n_f = D_FF // tf
    tiles_per_seq = SEQ // tm
    blk = tm * d * 4 * 2 + ADA_CHUNKS * d * 4 + 3 * d * tf * 2 + 2 * d * 4
    return pl.pallas_call(
        _ffn_kernel,
        out_shape=jax.ShapeDtypeStruct((N_TOK, d), F32),
        grid=(N_TOK // tm, n_f),
        in_specs=[
            pl.BlockSpec((tm, d), lambda i, k: (i, 0)),
            pl.BlockSpec((None, ADA_CHUNKS, d), lambda i, k: (i // tiles_per_seq, 0, 0)),
            pl.BlockSpec((d, tf), lambda i, k: (0, k)),
            pl.BlockSpec((d, tf), lambda i, k: (0, k + n_f)),
            pl.BlockSpec((tf, d), lambda i, k: (k, 0)),
            pl.BlockSpec((1, d), lambda i, k: (0, 0)),
            pl.BlockSpec((1, d), lambda i, k: (0, 0)),
        ],
        out_specs=pl.BlockSpec((tm, d), lambda i, k: (i, 0)),
        scratch_shapes=[pltpu.VMEM((tm, d), BF16)],
        compiler_params=_cparams(("parallel", "arbitrary"),
                                 _vmem_limit(blk, tm * d * 2, 6 * tm * tf * 4 + tm * ACC_TN * 4)),
        name="dense_ffn",
    )(x, mod, w_gu, w_gu, w_down, lng, lnb)


def _rope(r, c, s_up, s_dn):
    outs = []
    for j in range(r.shape[1] // HEAD_DIM):
        xc = r[:, j * HEAD_DIM:(j + 1) * HEAD_DIM]
        outs.append(xc * c + pltpu.roll(xc, ROT_HALF, 1) * s_up
                    + pltpu.roll(xc, HEAD_DIM - ROT_HALF, 1) * s_dn)
    return jnp.concatenate(outs, axis=1)


def _qkv_kernel(x_ref, modm_ref, modkv_ref, w_ref, c_ref, su_ref, sd_ref, o_ref,
                hq_scr, hkv_scr, *, n_q, n_k):
    n = pl.program_id(1)

    @pl.when(n == 0)
    def _():
        def prep(rows):
            x = x_ref[rows, :]
            hq_scr[rows, :] = (x * (1.0 + modm_ref[1:2, :]) + modm_ref[0:1, :]).astype(BF16)
            hkv_scr[rows, :] = (x * (1.0 + modkv_ref[1:2, :]) + modkv_ref[0:1, :]).astype(BF16)
        _for_row_chunks(x_ref.shape[0], prep)

    @pl.when(n < n_q)
    def _():
        r = jnp.dot(hq_scr[...], w_ref[...], preferred_element_type=F32)
        r = _rope(r, c_ref[...], su_ref[...], sd_ref[...]) * Q_PRESCALE
        o_ref[...] = r.astype(BF16)

    @pl.when((n >= n_q) & (n < n_q + n_k))
    def _():
        r = jnp.dot(hkv_scr[...], w_ref[...], preferred_element_type=F32)
        o_ref[...] = _rope(r, c_ref[...], su_ref[...], sd_ref[...]).astype(BF16)

    @pl.when(n >= n_q + n_k)
    def _():
        o_ref[...] = jnp.dot(hkv_scr[...], w_ref[...],
                             preferred_element_type=F32).astype(BF16)


def _qkv_call(x, modm, modkv, w_qkv, rope_c, rope_up, rope_dn):
    tm, tn, d = QKV_TM, QKV_TN, D_MODEL
    n_out = w_qkv.shape[1]
    tiles_per_seq = SEQ // tm
    blk = tm * d * 4 + (ADA_CHUNKS + 2) * d * 4 + d * tn * 2 + 3 * tm * HEAD_DIM * 4 + tm * tn * 2
    kern = functools.partial(_qkv_kernel, n_q=D_MODEL // tn, n_k=D_MODEL // tn)
    return pl.pallas_call(
        kern,
        out_shape=jax.ShapeDtypeStruct((N_TOK, n_out), BF16),
        grid=(N_TOK // tm, n_out // tn),
        in_specs=[
            pl.BlockSpec((tm, d), lambda i, n: (i, 0)),
            pl.BlockSpec((None, ADA_CHUNKS, d), lambda i, n: (i // tiles_per_seq, 0, 0)),
            pl.BlockSpec((None, 2, d), lambda i, n: (i // tiles_per_seq, 0, 0)),
            pl.BlockSpec((d, tn), lambda i, n: (0, n)),
            pl.BlockSpec((tm, HEAD_DIM), lambda i, n: (i, 0)),
            pl.BlockSpec((tm, HEAD_DIM), lambda i, n: (i, 0)),
            pl.BlockSpec((tm, HEAD_DIM), lambda i, n: (i, 0)),
        ],
        out_specs=pl.BlockSpec((tm, tn), lambda i, n: (i, n)),
        scratch_shapes=[pltpu.VMEM((tm, d), BF16), pltpu.VMEM((tm, d), BF16)],
        compiler_params=_cparams(("parallel", "arbitrary"),
                                 _vmem_limit(blk, 2 * tm * d * 2, 4 * tm * tn * 4)),
        name="qkv_proj",
    )(x, modm, modkv, w_qkv, rope_c, rope_up, rope_dn)


def _attn_kernel(q_ref, k_ref, v_ref, lam_ref, g_ref, o_ref, m_scr, l_scr, acc_scr,
                 *, lambda_init):
    i = pl.program_id(2)
    tq = q_ref.shape[0]
    tk = ATT_TK
    m_scr[...] = jnp.full(m_scr.shape, -jnp.inf, F32)
    l_scr[...] = jnp.zeros(l_scr.shape, F32)
    acc_scr[...] = jnp.zeros(acc_scr.shape, F32)

    def chunk(start, width, masked):
        kc = k_ref[pl.ds(start, width), :]
        vc = v_ref[pl.ds(start, width), :]
        for c in range(2):
            qc = q_ref[:, c * HEAD_DIM:(c + 1) * HEAD_DIM]
            s = lax.dot_general(qc, kc[:, c * HEAD_DIM:(c + 1) * HEAD_DIM],
                                (((1,), (1,)), ((), ())), preferred_element_type=F32)
            if masked:
                row = lax.broadcasted_iota(jnp.int32, (tq, width), 0)
                col = lax.broadcasted_iota(jnp.int32, (tq, width), 1)
                s = jnp.where(col <= row, s, -jnp.inf)
            lanes = [s[:, t * V7X_LANES:(t + 1) * V7X_LANES] for t in range(width // V7X_LANES)]
            m_prev = m_scr[c]
            m_new = jnp.maximum(m_prev, jnp.max(functools.reduce(jnp.maximum, lanes),
                                                axis=1, keepdims=True))
            a = jnp.exp2(m_prev - m_new)
            ps = [jnp.exp2(x - m_new) for x in lanes]
            l_scr[c] = a * l_scr[c] + functools.reduce(jnp.add, ps)
            p = jnp.concatenate([x.astype(BF16) for x in ps], axis=1)
            acc_scr[c] = (jnp.concatenate([a, a], axis=1) * acc_scr[c]
                          + jnp.dot(p, vc, preferred_element_type=F32))
            m_scr[c] = m_new

    per_wide = tk // tq
    n_wide = i // per_wide

    def body(j, carry):
        chunk(pl.multiple_of(2 * j * tk, tk), tk, False)
        chunk(pl.multiple_of((2 * j + 1) * tk, tk), tk, False)
        return carry

    lax.fori_loop(0, n_wide // 2, body, 0)

    @pl.when(n_wide % 2 == 1)
    def _():
        chunk(pl.multiple_of((n_wide - 1) * tk, tk), tk, False)

    for r in range(per_wide - 1):
        @pl.when(i - n_wide * per_wide > r)
        def _():
            chunk(pl.multiple_of((n_wide * per_wide + r) * tq, tq), tq, False)
    chunk(pl.multiple_of(i * tq, tq), tq, True)

    lam = (jnp.exp(jnp.sum(lam_ref[0:1, :] * lam_ref[1:2, :], axis=1, keepdims=True))
           - jnp.exp(jnp.sum(lam_ref[2:3, :] * lam_ref[3:4, :], axis=1, keepdims=True))
           + lambda_init)
    l0 = jnp.sum(l_scr[0], axis=1, keepdims=True)
    l1 = jnp.sum(l_scr[1], axis=1, keepdims=True)
    o = acc_scr[0] / l0 - lam * (acc_scr[1] / l1)
    ms = jnp.mean(o * o, axis=-1, keepdims=True)
    o_ref[...] = (o * lax.rsqrt(ms + LN_EPS) * g_ref[...] * (1.0 - lambda_init)).astype(BF16)


def _attn_call(qkv, lam_vecs, subln_g, lambda_init):
    assert ATT_TK % ATT_TQ == 0
    tq = ATT_TQ
    n_q = SEQ // tq
    kern = functools.partial(_attn_kernel, lambda_init=lambda_init)
    blk = 2 * tq * V_DIM * 2 + 2 * SEQ * V_DIM * 2 + 4 * HEAD_DIM * 4 + V_DIM * 4
    scr = 2 * tq * V_DIM * 4 + 4 * tq * V7X_LANES * 4
    return pl.pallas_call(
        kern,
        out_shape=jax.ShapeDtypeStruct((N_TOK, D_MODEL), BF16),
        grid=(BATCH, N_HEADS, n_q),
        in_specs=[
            pl.BlockSpec((tq, V_DIM), lambda b, h, i: (b * n_q + i, h)),
            pl.BlockSpec((SEQ, V_DIM), lambda b, h, i: (b, N_HEADS + h)),
            pl.BlockSpec((SEQ, V_DIM), lambda b, h, i: (b, 2 * N_HEADS + h)),
            pl.BlockSpec((4, HEAD_DIM), lambda b, h, i: (0, 0)),
            pl.BlockSpec((1, V_DIM), lambda b, h, i: (0, 0)),
        ],
        out_specs=pl.BlockSpec((tq, V_DIM), lambda b, h, i: (b * n_q + i, h)),
        scratch_shapes=[pltpu.VMEM((2, tq, V7X_LANES), F32), pltpu.VMEM((2, tq, V7X_LANES), F32),
                        pltpu.VMEM((2, tq, V_DIM), F32)],
        compiler_params=_cparams(("parallel", "parallel", "arbitrary"),
                                 _vmem_limit(blk, scr, 8 * tq * ATT_TK * 4)),
        name="diff_attn",
    )(qkv, qkv, qkv, lam_vecs, subln_g)


def _wo_kernel(a_ref, w_ref, x_ref, mod_ref, lng_ref, lnb_ref, wr_ref,
               o_ref, h_ref, lg_ref):
    k = pl.program_id(1)
    tm = x_ref.shape[0]

    @pl.when(k == 0)
    def _():
        def prep(rows):
            o_ref[rows, :] = jnp.zeros((EPI_ROWS, o_ref.shape[1]), F32)
        _for_row_chunks(tm, prep)

    _accumulate_dot(o_ref, a_ref[...], w_ref)

    @pl.when(k == pl.num_programs(1) - 1)
    def _():
        def fin(rows):
            z = ALPHA * x_ref[rows, :] + (1.0 + mod_ref[2:3, :]) * o_ref[rows, :]
            xn = _layer_norm(z, lng_ref[...], lnb_ref[...])
            o_ref[rows, :] = xn
            h = xn * (1.0 + mod_ref[4:5, :]) + mod_ref[3:4, :]
            bits = lax.bitcast_convert_type(h.astype(BF16).astype(F32), jnp.uint32)
            half = bits.shape[1] // 2
            h_ref[rows, :] = bits[:, :half] | (bits[:, half:] >> 16)
            lane = lax.broadcasted_iota(jnp.int32, (EPI_ROWS, ROUTER_PAD), 1)
            lg = jnp.zeros((EPI_ROWS, ROUTER_PAD), F32)
            for e in range(N_EXPERTS):
                val = jnp.sum(h * wr_ref[e:e + 1, :], axis=1, keepdims=True)
                lg = jnp.where(lane == e, val, lg)
            lg_ref[rows, :] = lg
        _for_row_chunks(tm, fin)


def _wo_call(attn, w_o, x, mod, lng, lnb, w_router_t):
    tm, tk, d = WO_TM, WO_TK, D_MODEL
    tiles_per_seq = SEQ // tm
    blk = (tm * tk * 2 + tk * d * 2 + tm * d * 4 * 2 + tm * d * 2 + ADA_CHUNKS * d * 4
           + 2 * d * 4 + tm * ROUTER_PAD * 4)
    return pl.pallas_call(
        _wo_kernel,
        out_shape=(jax.ShapeDtypeStruct((N_TOK, d), F32),
                   jax.ShapeDtypeStruct((N_TOK, d // 2), jnp.uint32),
                   jax.ShapeDtypeStruct((N_TOK, ROUTER_PAD), F32)),
        grid=(N_TOK // tm, d // tk),
        in_specs=[
            pl.BlockSpec((tm, tk), lambda i, k: (i, k)),
            pl.BlockSpec((tk, d), lambda i, k: (k, 0)),
            pl.BlockSpec((tm, d), lambda i, k: (i, 0)),
            pl.BlockSpec((None, ADA_CHUNKS, d), lambda i, k: (i // tiles_per_seq, 0, 0)),
            pl.BlockSpec((1, d), lambda i, k: (0, 0)),
            pl.BlockSpec((1, d), lambda i, k: (0, 0)),
            pl.BlockSpec((N_EXPERTS, d), lambda i, k: (0, 0)),
        ],
        out_specs=(pl.BlockSpec((tm, d), lambda i, k: (i, 0)),
                   pl.BlockSpec((tm, d // 2), lambda i, k: (i, 0)),
                   pl.BlockSpec((tm, ROUTER_PAD), lambda i, k: (i, 0))),
        compiler_params=_cparams(("parallel", "arbitrary"),
                                 _vmem_limit(blk, 2 * N_EXPERTS * d * 4,
                                             tm * ACC_TN * 4 + 6 * EPI_ROWS * d * 4)),
        name="attn_out_proj",
    )(attn, w_o, x, mod, lng, lnb, w_router_t)


def _gather_row_copy(src_hbm, row, dst_buf, dst_row, sem):
    return pltpu.make_async_copy(src_hbm.at[pl.ds(row, 1), :], dst_buf.at[pl.ds(dst_row, 1), :], sem)


def _moe_kernel(be_ref, nu_ref, rt_ref, h_hbm, wg_ref, wu_ref, wd_ref, o_ref,
                hbuf, h_scr, sem):
    i = pl.program_id(0)
    k = pl.program_id(1)
    n_f = pl.num_programs(1)
    tm = h_scr.shape[0]
    half = h_scr.shape[1] // 2
    per_step = tm // MOE_N_F
    n_used = nu_ref[0]

    def start_rows(blk, first, count):
        def body(r, carry):
            row = first + r
            _gather_row_copy(h_hbm, rt_ref[blk * tm + row], hbuf, row, sem.at[0]).start()
            return carry
        lax.fori_loop(0, count, body, 0, unroll=DMA_UNROLL)

    def wait_rows():
        def body(r, carry):
            _gather_row_copy(h_hbm, 0, hbuf, r, sem.at[0]).wait()
            return carry
        lax.fori_loop(0, tm, body, 0, unroll=DMA_UNROLL)

    @pl.when((i >= n_used) & (k == 0))
    def _():
        def clear(rows):
            o_ref[rows, :] = jnp.zeros((EPI_ROWS, o_ref.shape[1]), F32)
        _for_row_chunks(tm, clear)

    @pl.when(i < n_used)
    def _():
        @pl.when((i == 0) & (k == 0))
        def _():
            start_rows(0, 0, tm)

        @pl.when(k == 0)
        def _():
            wait_rows()

            def prep(rows):
                w = hbuf[rows, :]
                hi = lax.bitcast_convert_type(w & jnp.uint32(0xFFFF0000), F32)
                lo = lax.bitcast_convert_type(w << 16, F32)
                h_scr[rows, 0:half] = hi.astype(BF16)
                h_scr[rows, half:] = lo.astype(BF16)
                o_ref[rows, :] = jnp.zeros((EPI_ROWS, o_ref.shape[1]), F32)
            _for_row_chunks(tm, prep)

        nxt = jnp.minimum(i + 1, n_used - 1)
        for r in range(per_step):
            row = k * per_step + r
            _gather_row_copy(h_hbm, rt_ref[nxt * tm + row], hbuf, row, sem.at[0]).start()

        h = h_scr[...]
        g = jnp.dot(h, wg_ref[...], preferred_element_type=F32)
        u = jnp.dot(h, wu_ref[...], preferred_element_type=F32)
        _accumulate_dot(o_ref, (_silu(g) * u).astype(BF16), wd_ref)

        @pl.when((i == n_used - 1) & (k == n_f - 1))
        def _():
            wait_rows()


def _moe_call(blk_expert, n_used, row_tok, h_packed, w_g, w_u, w_down):
    tm, tf, d = MOE_TM, MOE_TF, D_MODEL
    n_f = MOE_N_F

    def row_blk(i, nu):
        return jnp.minimum(i, nu[0] - 1)

    def f_blk(i, k, nu):
        return jnp.where(i < nu[0], k, n_f - 1)

    blk = 3 * d * tf * 2 + tm * d * 4
    scr = tm * (d // 2) * 4 + tm * d * 2
    grid_spec = pltpu.PrefetchScalarGridSpec(
        num_scalar_prefetch=3,
        grid=(MOE_N_BLK, n_f),
        in_specs=[
            pl.BlockSpec(memory_space=pl.ANY),
            pl.BlockSpec((None, d, tf),
                         lambda i, k, be, nu, rt: (be[row_blk(i, nu)], 0, f_blk(i, k, nu))),
            pl.BlockSpec((None, d, tf),
                         lambda i, k, be, nu, rt: (be[row_blk(i, nu)], 0, f_blk(i, k, nu))),
            pl.BlockSpec((None, tf, d),
                         lambda i, k, be, nu, rt: (be[row_blk(i, nu)], f_blk(i, k, nu), 0)),
        ],
        out_specs=pl.BlockSpec((tm, d), lambda i, k, be, nu, rt: (i, 0)),
        scratch_shapes=[pltpu.VMEM((tm, d // 2), jnp.uint32), pltpu.VMEM((tm, d), BF16),
                        pltpu.SemaphoreType.DMA((1,))],
    )
    return pl.pallas_call(
        _moe_kernel,
        out_shape=jax.ShapeDtypeStruct((MOE_N_ROWS, d), F32),
        grid_spec=grid_spec,
        compiler_params=pltpu.CompilerParams(
            dimension_semantics=("arbitrary", "arbitrary"),
            vmem_limit_bytes=_vmem_limit(blk, scr, 6 * tm * tf * 4 + tm * ACC_TN * 4),
            disable_bounds_checks=True),
        name="moe_ffn",
    )(blk_expert, n_used, row_tok, h_packed, w_g, w_u, w_down)


def _final_kernel(dest_ref, x_ref, y_hbm, tw_ref, mod_ref, lng_ref, lnb_ref, o_ref, ybuf, sem):
    i = pl.program_id(0)
    tm = x_ref.shape[0]
    slot = i % 2

    def start_tile(tile, dst_slot):
        def body(t, carry):
            for j in range(TOP_K):
                _gather_row_copy(y_hbm, dest_ref[(tile * tm + t) * TOP_K + j],
                                 ybuf.at[dst_slot, j], t, sem.at[dst_slot]).start()
            return carry
        lax.fori_loop(0, tm, body, 0, unroll=DMA_UNROLL)

    def wait_tile(dst_slot):
        def body(t, carry):
            for j in range(TOP_K):
                _gather_row_copy(y_hbm, 0, ybuf.at[dst_slot, j], t, sem.at[dst_slot]).wait()
            return carry
        lax.fori_loop(0, tm, body, 0, unroll=DMA_UNROLL)

    @pl.when(i == 0)
    def _():
        start_tile(0, 0)

    @pl.when(i + 1 < pl.num_programs(0))
    def _():
        start_tile(i + 1, 1 - slot)

    wait_tile(slot)

    def fin(rows):
        y = (tw_ref[rows, 0:1] * ybuf[slot, 0, rows, :]
             + tw_ref[rows, 1:2] * ybuf[slot, 1, rows, :])
        z = ALPHA * x_ref[rows, :] + (1.0 + mod_ref[5:6, :]) * y
        o_ref[rows, :] = _layer_norm(z, lng_ref[...], lnb_ref[...])
    _for_row_chunks(tm, fin)


def _final_call(dest, x, y_rows, top_w, mod, lng, lnb):
    tm, d = FIN_TM, D_MODEL
    tiles_per_seq = SEQ // tm
    blk = 2 * tm * d * 4 + tm * V7X_LANES * 4 + ADA_CHUNKS * d * 4 + 2 * d * 4
    grid_spec = pltpu.PrefetchScalarGridSpec(
        num_scalar_prefetch=1,
        grid=(N_TOK // tm,),
        in_specs=[
            pl.BlockSpec((tm, d), lambda i, de: (i, 0)),
            pl.BlockSpec(memory_space=pl.ANY),
            pl.BlockSpec((tm, TOP_K), lambda i, de: (i, 0)),
            pl.BlockSpec((None, ADA_CHUNKS, d), lambda i, de: (i // tiles_per_seq, 0, 0)),
            pl.BlockSpec((1, d), lambda i, de: (0, 0)),
            pl.BlockSpec((1, d), lambda i, de: (0, 0)),
        ],
        out_specs=pl.BlockSpec((tm, d), lambda i, de: (i, 0)),
        scratch_shapes=[pltpu.VMEM((2, TOP_K, tm, d), F32), pltpu.SemaphoreType.DMA((2,))],
    )
    return pl.pallas_call(
        _final_kernel,
        out_shape=jax.ShapeDtypeStruct((N_TOK, d), F32),
        grid_spec=grid_spec,
        compiler_params=pltpu.CompilerParams(
            dimension_semantics=("arbitrary",),
            vmem_limit_bytes=_vmem_limit(blk, 2 * TOP_K * tm * d * 4, 8 * EPI_ROWS * d * 4),
            disable_bounds_checks=True),
        name="final_ln",
    )(dest, x, y_rows, top_w, mod, lng, lnb)


def _cast_split_kernel(w_ref, g_ref, u_ref):
    n = g_ref.shape[1]

    def cast(rows):
        g_ref[rows, :] = w_ref[rows, 0:n].astype(BF16)
        u_ref[rows, :] = w_ref[rows, n:].astype(BF16)
    _for_row_chunks(w_ref.shape[0], cast)


def _cast_split_call(w):
    r, n2 = w.shape
    n = n2 // 2
    tm = CAST_TM
    blk = tm * n2 * 4 + 2 * tm * n * 2
    return pl.pallas_call(
        _cast_split_kernel,
        out_shape=(jax.ShapeDtypeStruct((r, n), BF16), jax.ShapeDtypeStruct((r, n), BF16)),
        grid=(r // tm,),
        in_specs=[pl.BlockSpec((tm, n2), lambda i: (i, 0))],
        out_specs=(pl.BlockSpec((tm, n), lambda i: (i, 0)),
                   pl.BlockSpec((tm, n), lambda i: (i, 0))),
        compiler_params=_cparams(("parallel",), _vmem_limit(blk, 0, 4 * EPI_ROWS * n2 * 4)),
        name="cast_gate_up",
    )(w)


def _route(logits):
    top_logit, top_idx = lax.top_k(logits, TOP_K)
    top_w = jax.nn.softmax(top_logit, axis=-1)
    e_flat = top_idx.reshape(MOE_N_ASG).astype(jnp.int32)
    tok_flat = jnp.arange(MOE_N_ASG, dtype=jnp.int32) // TOP_K
    onehot = (e_flat[:, None] == jnp.arange(N_EXPERTS, dtype=jnp.int32)[None, :]).astype(jnp.int32)
    csum = jnp.cumsum(onehot, axis=0)
    rank = jnp.sum((csum - onehot) * onehot, axis=1)
    counts = csum[-1]
    padded = (counts + MOE_TM - 1) // MOE_TM * MOE_TM
    padded_ends = jnp.cumsum(padded)
    padded_starts = padded_ends - padded
    dest = padded_starts[e_flat] + rank
    n_used = (padded_ends[-1:] // MOE_TM).astype(jnp.int32)
    blk_start = jnp.arange(MOE_N_BLK, dtype=jnp.int32) * MOE_TM
    blk_expert = jnp.minimum(jnp.searchsorted(padded_ends, blk_start, side='right'),
                             N_EXPERTS - 1).astype(jnp.int32)
    row_tok = jnp.zeros((MOE_N_ROWS,), jnp.int32).at[dest].set(
        tok_flat, unique_indices=True, mode='promise_in_bounds')
    return blk_expert, n_used, dest, row_tok, top_w


def _rope_tables(positions):
    inv_freq = ROPE_THETA ** (-jnp.arange(0, ROT_DIM, 2, dtype=F32) / ROT_DIM)
    ang = positions.astype(F32)[..., None] * inv_freq
    cos = jnp.cos(ang).reshape(N_TOK, ROT_HALF)
    sin = jnp.sin(ang).reshape(N_TOK, ROT_HALF)
    rest = HEAD_DIM - ROT_DIM
    c = jnp.concatenate([cos, cos, jnp.ones((N_TOK, rest), F32)], axis=1)
    s_up = jnp.concatenate([jnp.zeros((N_TOK, ROT_HALF), F32), sin,
                            jnp.zeros((N_TOK, rest), F32)], axis=1)
    s_dn = jnp.concatenate([-sin, jnp.zeros((N_TOK, HEAD_DIM - ROT_HALF), F32)], axis=1)
    return c, s_up, s_dn


def kernel(x, c, positions, ada_w, ada_b, ln_g, ln_b, kv_ada_w, kv_ada_b, w_pool, pool_scale,
           w_kv, w_q, w_o, lam_q1, lam_k1, lam_q2, lam_k2, subln_g, ffn_w_gu, ffn_w_down,
           router_w, moe_w_gu, moe_w_down):
    d = D_MODEL
    xt = x.reshape(N_TOK, d)

    c8 = jnp.pad(c, ((0, V7X_SUBLANES - BATCH), (0, 0)))
    mods = _ada_call(c8, ada_w, ada_b)[:, :BATCH].reshape(DEPTH, BATCH, ADA_CHUNKS, d)
    mod_kv = _ada_call(c8, kv_ada_w[None], kv_ada_b[None])[0, :BATCH].reshape(BATCH, 2, d)

    x1 = _pool_call(xt, mods[0], w_pool[0].astype(BF16), pool_scale[0][None],
                    ln_g[0, 0][None], ln_b[0, 0][None])
    x2 = _ffn_call(x1, mods[0], ffn_w_gu[0].astype(BF16), ffn_w_down[0].astype(BF16),
                   ln_g[0, 1][None], ln_b[0, 1][None])

    rope_c, rope_up, rope_dn = _rope_tables(positions)
    w_qkv = jnp.concatenate([w_q[0], w_kv], axis=1).astype(BF16)
    qkv = _qkv_call(x2, mods[1], mod_kv, w_qkv, rope_c, rope_up, rope_dn)
    lam_vecs = jnp.stack([lam_q1[0], lam_k1[0], lam_q2[0], lam_k2[0]], axis=0)
    lambda_init = 0.8 - 0.6 * math.exp(-0.3 * 1)
    attn = _attn_call(qkv, lam_vecs, subln_g[0][None], lambda_init)
    w_router_t = router_w[0].T
    x3, h3, logits = _wo_call(attn, w_o[0].astype(BF16), x2, mods[1],
                              ln_g[1, 0][None], ln_b[1, 0][None], w_router_t)

    blk_expert, n_used, dest, row_tok, top_w = _route(logits[:, :N_EXPERTS])
    w_g, w_u = _cast_split_call(moe_w_gu[0].reshape(N_EXPERTS * d, 2 * D_FF_EXPERT))
    y_rows = _moe_call(blk_expert, n_used, row_tok, h3,
                       w_g.reshape(N_EXPERTS, d, D_FF_EXPERT),
                       w_u.reshape(N_EXPERTS, d, D_FF_EXPERT),
                       moe_w_down[0].astype(BF16))
    out = _final_call(dest, x3, y_rows, top_w, mods[1], ln_g[1, 1][None], ln_b[1, 1][None])
    return out.reshape(BATCH, SEQ, d)
```

```python
import functools
import math

import jax
import jax.numpy as jnp
from jax import lax
from jax.experimental import pallas as pl
from jax.experimental.pallas import tpu as pltpu

F32 = jnp.float32
BF16 = jnp.bfloat16

D_MODEL = 4096
BATCH = 2
SEQ = 8192
N_TOK = BATCH * SEQ
DEPTH = 2
POOL_WINDOWS = (2, 4, 8, 16)
POOL_GROUP_DIM = D_MODEL // len(POOL_WINDOWS)
POOL_HALO = 16
N_HEADS = 16
HEAD_DIM = 128
V_DIM = 2 * HEAD_DIM
ROT_DIM = HEAD_DIM // 4
ROT_HALF = ROT_DIM // 2
ROPE_THETA = 500000.0
D_FF = 11008
N_EXPERTS = 8
TOP_K = 2
D_FF_EXPERT = D_MODEL
LN_EPS = 1e-5
ALPHA = (2.0 * DEPTH) ** 0.25
ADA_CHUNKS = 6
LOG2E = math.log2(math.e)
Q_PRESCALE = HEAD_DIM ** -0.5 * LOG2E

V7X_VMEM_BYTES = 64 * 1024 * 1024
V7X_LANES = 128
V7X_SUBLANES = 8

ADA_TN = 1024
ADA_TK = 512
POOL_TS = 256
FFN_TM = 512
FFN_TF = 256
QKV_TM = 512
QKV_TN = 1024
ATT_TQ = 512
ATT_TK = 512
ATT_UNROLL = 4
WO_TM = 512
WO_TK = 512
MOE_TM = 512
MOE_TF = 512
MOE_N_F = D_FF_EXPERT // MOE_TF
FIN_TM = 256
CAST_TM = 512
ACC_TN = 1024
EPI_ROWS = 64
DMA_UNROLL = 8
ROUTER_PAD = V7X_LANES

MOE_N_ASG = N_TOK * TOP_K
MOE_N_BLK = MOE_N_ASG // MOE_TM + N_EXPERTS
MOE_N_ROWS = MOE_N_BLK * MOE_TM


def _vmem_limit(pipelined_bytes, resident_bytes=0, temp_bytes=0):
    need = 2 * pipelined_bytes + resident_bytes + temp_bytes + (2 << 20)
    assert need <= V7X_VMEM_BYTES - (2 << 20), need
    return int(need)


def _cparams(sem, vmem):
    return pltpu.CompilerParams(dimension_semantics=sem, vmem_limit_bytes=vmem)


def _layer_norm(z, g, b):
    mu = jnp.mean(z, axis=-1, keepdims=True)
    zc = z - mu
    var = jnp.mean(zc * zc, axis=-1, keepdims=True)
    return zc * lax.rsqrt(var + LN_EPS) * g + b


def _silu(x):
    return x / (1.0 + jnp.exp(-x))


def _for_row_chunks(n_rows, fn):
    def body(r, carry):
        fn(pl.ds(pl.multiple_of(r * EPI_ROWS, EPI_ROWS), EPI_ROWS))
        return carry
    lax.fori_loop(0, n_rows // EPI_ROWS, body, 0)


def _accumulate_dot(o_ref, a, w_ref):
    for j in range(o_ref.shape[1] // ACC_TN):
        cols = slice(j * ACC_TN, (j + 1) * ACC_TN)
        o_ref[:, cols] += jnp.dot(a, w_ref[:, cols], preferred_element_type=F32)


def _ada_kernel(c_ref, w_ref, b_ref, o_ref):
    acc = jnp.zeros(o_ref.shape, F32) + b_ref[...]
    for kk in range(c_ref.shape[1] // ADA_TK):
        ks = slice(kk * ADA_TK, (kk + 1) * ADA_TK)
        cond = _silu(c_ref[:, ks]).astype(BF16)
        acc = acc + jnp.dot(cond, w_ref[ks, :].astype(BF16), preferred_element_type=F32)
    o_ref[...] = acc


def _ada_call(c8, w, b):
    n_l, d, n = w.shape
    tn = ADA_TN
    blk = d * tn * 4 + 8 * d * 4 + 8 * tn * 4 + tn * 4
    return pl.pallas_call(
        _ada_kernel,
        out_shape=jax.ShapeDtypeStruct((n_l, 8, n), F32),
        grid=(n_l, n // tn),
        in_specs=[
            pl.BlockSpec((8, d), lambda l, j: (0, 0)),
            pl.BlockSpec((None, d, tn), lambda l, j: (l, 0, j)),
            pl.BlockSpec((None, 1, tn), lambda l, j: (l, 0, j)),
        ],
        out_specs=pl.BlockSpec((None, 8, tn), lambda l, j: (l, 0, j)),
        compiler_params=_cparams(("parallel", "parallel"),
                                 _vmem_limit(blk, temp_bytes=4 * ADA_TK * tn * 4)),
        name="ada_mod",
    )(c8, w, b.reshape(n_l, 1, n))


def _pool_kernel(x_ref, halo_ref, mod_ref, wp_ref, ps_ref, lng_ref, lnb_ref, o_ref):
    i = pl.program_id(1)
    ts = x_ref.shape[0]
    t1 = (i * ts + 1 + lax.broadcasted_iota(jnp.int32, (ts, 1), 0)).astype(F32)
    for g, w in enumerate(POOL_WINDOWS):
        cols = slice(g * POOL_GROUP_DIM, (g + 1) * POOL_GROUP_DIM)
        sh = mod_ref[0:1, cols]
        sc = mod_ref[1:2, cols]
        gate = mod_ref[2:3, cols]
        x = x_ref[:, cols]
        h = x * (1.0 + sc) + sh
        hh = jnp.where(i > 0, halo_ref[:, cols] * (1.0 + sc) + sh, 0.0)
        s = jnp.concatenate([hh, h], axis=0)
        span = 1
        while span < w:
            s = s + pltpu.roll(s, span, 0)
            span *= 2
        win = s[POOL_HALO:, :]
        pooled = win / jnp.minimum(t1, float(w)) - h
        mixed = jnp.dot(pooled.astype(BF16), wp_ref[g], preferred_element_type=F32)
        o_ref[:, cols] = ALPHA * x + (1.0 + gate) * (mixed * ps_ref[:, cols])

    def ln_rows(rows):
        o_ref[rows, :] = _layer_norm(o_ref[rows, :], lng_ref[...], lnb_ref[...])
    _for_row_chunks(ts, ln_rows)


def _pool_call(x, mod, wp, ps, lng, lnb):
    ts = POOL_TS
    n_s = SEQ // ts
    halo_per_tile = ts // POOL_HALO
    d = D_MODEL
    blk = ts * d * 4 * 2 + POOL_HALO * d * 4 + ADA_CHUNKS * d * 4 + 3 * d * 4
    return pl.pallas_call(
        _pool_kernel,
        out_shape=jax.ShapeDtypeStruct((N_TOK, d), F32),
        grid=(BATCH, n_s),
        in_specs=[
            pl.BlockSpec((ts, d), lambda b, i: (b * n_s + i, 0)),
            pl.BlockSpec((POOL_HALO, d),
                         lambda b, i: (jnp.maximum((b * n_s + i) * halo_per_tile - 1, 0), 0)),
            pl.BlockSpec((None, ADA_CHUNKS, d), lambda b, i: (b, 0, 0)),
            pl.BlockSpec(wp.shape, lambda b, i: (0, 0, 0), pipeline_mode=pl.Buffered(1)),
            pl.BlockSpec((1, d), lambda b, i: (0, 0)),
            pl.BlockSpec((1, d), lambda b, i: (0, 0)),
            pl.BlockSpec((1, d), lambda b, i: (0, 0)),
        ],
        out_specs=pl.BlockSpec((ts, d), lambda b, i: (b * n_s + i, 0)),
        compiler_params=_cparams(
            ("parallel", "parallel"),
            _vmem_limit(blk, wp.size * 2, 8 * (ts + POOL_HALO) * POOL_GROUP_DIM * 4)),
        name="pool_mixer",
    )(x, x, mod, wp, ps, lng, lnb)


def _ffn_kernel(x_ref, mod_ref, wg_ref, wu_ref, wd_ref, lng_ref, lnb_ref, o_ref, h_scr):
    k = pl.program_id(1)
    tm = x_ref.shape[0]

    @pl.when(k == 0)
    def _():
        def prep(rows):
            h_scr[rows, :] = (x_ref[rows, :] * (1.0 + mod_ref[4:5, :])
                              + mod_ref[3:4, :]).astype(BF16)
            o_ref[rows, :] = jnp.zeros((EPI_ROWS, o_ref.shape[1]), F32)
        _for_row_chunks(tm, prep)

    h = h_scr[...]
    g = jnp.dot(h, wg_ref[...], preferred_element_type=F32)
    u = jnp.dot(h, wu_ref[...], preferred_element_type=F32)
    _accumulate_dot(o_ref, (_silu(g) * u).astype(BF16), wd_ref)

    @pl.when(k == pl.num_programs(1) - 1)
    def _():
        def fin(rows):
            z = ALPHA * x_ref[rows, :] + (1.0 + mod_ref[5:6, :]) * o_ref[rows, :]
            o_ref[rows, :] = _layer_norm(z, lng_ref[...], lnb_ref[...])
        _for_row_chunks(tm, fin)


def _ffn_call(x, mod, w_gu, w_down, lng, lnb):
    tm, tf, d = FFN_TM, FFN_TF, D_MODEL
    n_f = D_FF // tf
    tiles_per_seq = SEQ // tm
    blk = tm * d * 4 * 2 + ADA_CHUNKS * d * 4 + 3 * d * tf * 2 + 2 * d * 4
    return pl.pallas_call(
        _ffn_kernel,
        out_shape=jax.ShapeDtypeStruct((N_TOK, d), F32),
        grid=(N_TOK // tm, n_f),
        in_specs=[
            pl.BlockSpec((tm, d), lambda i, k: (i, 0)),
            pl.BlockSpec((None, ADA_CHUNKS, d), lambda i, k: (i // tiles_per_seq, 0, 0)),
            pl.BlockSpec((d, tf), lambda i, k: (0, k)),
            pl.BlockSpec((d, tf), lambda i, k: (0, k + n_f)),
            pl.BlockSpec((tf, d), lambda i, k: (k, 0)),
            pl.BlockSpec((1, d), lambda i, k: (0, 0)),
            pl.BlockSpec((1, d), lambda i, k: (0, 0)),
        ],
        out_specs=pl.BlockSpec((tm, d), lambda i, k: (i, 0)),
        scratch_shapes=[pltpu.VMEM((tm, d), BF16)],
        compiler_params=_cparams(("parallel", "arbitrary"),
                                 _vmem_limit(blk, tm * d * 2, 6 * tm * tf * 4 + tm * ACC_TN * 4)),
        name="dense_ffn",
    )(x, mod, w_gu, w_gu, w_down, lng, lnb)


def _rope(r, c, s_up, s_dn):
    outs = []
    for j in range(r.shape[1] // HEAD_DIM):
        xc = r[:, j * HEAD_DIM:(j + 1) * HEAD_DIM]
        outs.append(xc * c + pltpu.roll(xc, ROT_HALF, 1) * s_up
                    + pltpu.roll(xc, HEAD_DIM - ROT_HALF, 1) * s_dn)
    return jnp.concatenate(outs, axis=1)


def _qkv_kernel(x_ref, modm_ref, modkv_ref, w_ref, c_ref, su_ref, sd_ref, o_ref,
                hq_scr, hkv_scr, *, n_q, n_k):
    n = pl.program_id(1)

    @pl.when(n == 0)
    def _():
        def prep(rows):
            x = x_ref[rows, :]
            hq_scr[rows, :] = (x * (1.0 + modm_ref[1:2, :]) + modm_ref[0:1, :]).astype(BF16)
            hkv_scr[rows, :] = (x * (1.0 + modkv_ref[1:2, :]) + modkv_ref[0:1, :]).astype(BF16)
        _for_row_chunks(x_ref.shape[0], prep)

    @pl.when(n < n_q)
    def _():
        r = jnp.dot(hq_scr[...], w_ref[...], preferred_element_type=F32)
        r = _rope(r, c_ref[...], su_ref[...], sd_ref[...]) * Q_PRESCALE
        o_ref[...] = r.astype(BF16)

    @pl.when((n >= n_q) & (n < n_q + n_k))
    def _():
        r = jnp.dot(hkv_scr[...], w_ref[...], preferred_element_type=F32)
        o_ref[...] = _rope(r, c_ref[...], su_ref[...], sd_ref[...]).astype(BF16)

    @pl.when(n >= n_q + n_k)
    def _():
        o_ref[...] = jnp.dot(hkv_scr[...], w_ref[...],
                             preferred_element_type=F32).astype(BF16)


def _qkv_call(x, modm, modkv, w_qkv, rope_c, rope_up, rope_dn):
    tm, tn, d = QKV_TM, QKV_TN, D_MODEL
    n_out = w_qkv.shape[1]
    tiles_per_seq = SEQ // tm
    blk = tm * d * 4 + (ADA_CHUNKS + 2) * d * 4 + d * tn * 2 + 3 * tm * HEAD_DIM * 4 + tm * tn * 2
    kern = functools.partial(_qkv_kernel, n_q=D_MODEL // tn, n_k=D_MODEL // tn)
    return pl.pallas_call(
        kern,
        out_shape=jax.ShapeDtypeStruct((N_TOK, n_out), BF16),
        grid=(N_TOK // tm, n_out // tn),
        in_specs=[
            pl.BlockSpec((tm, d), lambda i, n: (i, 0)),
            pl.BlockSpec((None, ADA_CHUNKS, d), lambda i, n: (i // tiles_per_seq, 0, 0)),
            pl.BlockSpec((None, 2, d), lambda i, n: (i // tiles_per_seq, 0, 0)),
            pl.BlockSpec((d, tn), lambda i, n: (0, n)),
            pl.BlockSpec((tm, HEAD_DIM), lambda i, n: (i, 0)),
            pl.BlockSpec((tm, HEAD_DIM), lambda i, n: (i, 0)),
            pl.BlockSpec((tm, HEAD_DIM), lambda i, n: (i, 0)),
        ],
        out_specs=pl.BlockSpec((tm, tn), lambda i, n: (i, n)),
        scratch_shapes=[pltpu.VMEM((tm, d), BF16), pltpu.VMEM((tm, d), BF16)],
        compiler_params=_cparams(("parallel", "arbitrary"),
                                 _vmem_limit(blk, 2 * tm * d * 2, 4 * tm * tn * 4)),
        name="qkv_proj",
    )(x, modm, modkv, w_qkv, rope_c, rope_up, rope_dn)


def _attn_kernel(q_ref, k_ref, v_ref, lam_ref, g_ref, o_ref, m_scr, l_scr, acc_scr,
                 *, lambda_init):
    i = pl.program_id(2)
    tq = q_ref.shape[0]
    tk = ATT_TK
    m_scr[...] = jnp.full(m_scr.shape, -jnp.inf, F32)
    l_scr[...] = jnp.zeros(l_scr.shape, F32)
    acc_scr[...] = jnp.zeros(acc_scr.shape, F32)

    def chunk(start, width, masked):
        kc = k_ref[pl.ds(start, width), :]
        vc = v_ref[pl.ds(start, width), :]
        for c in range(2):
            qc = q_ref[:, c * HEAD_DIM:(c + 1) * HEAD_DIM]
            s = lax.dot_general(qc, kc[:, c * HEAD_DIM:(c + 1) * HEAD_DIM],
                                (((1,), (1,)), ((), ())), preferred_element_type=F32)
            if masked:
                row = lax.broadcasted_iota(jnp.int32, (tq, width), 0)
                col = lax.broadcasted_iota(jnp.int32, (tq, width), 1)
                s = jnp.where(col <= row, s, -jnp.inf)
            lanes = [s[:, t * V7X_LANES:(t + 1) * V7X_LANES] for t in range(width // V7X_LANES)]
            m_prev = m_scr[c]
            m_new = jnp.maximum(m_prev, jnp.max(functools.reduce(jnp.maximum, lanes),
                                                axis=1, keepdims=True))
            a = jnp.exp2(m_prev - m_new)
            ps = [jnp.exp2(x - m_new) for x in lanes]
            l_scr[c] = a * l_scr[c] + functools.reduce(jnp.add, ps)
            p = jnp.concatenate([x.astype(BF16) for x in ps], axis=1)
            acc_scr[c] = (jnp.concatenate([a, a], axis=1) * acc_scr[c]
                          + jnp.dot(p, vc, preferred_element_type=F32))
            m_scr[c] = m_new

    n_groups = i // ATT_UNROLL

    def body(j, carry):
        for u in range(ATT_UNROLL):
            chunk(pl.multiple_of((ATT_UNROLL * j + u) * tk, tk), tk, False)
        return carry

    lax.fori_loop(0, n_groups, body, 0)

    for r in range(ATT_UNROLL):
        @pl.when(i - n_groups * ATT_UNROLL == r)
        def _():
            for u in range(r):
                chunk(pl.multiple_of((n_groups * ATT_UNROLL + u) * tk, tk), tk, False)
            chunk(pl.multiple_of(i * tk, tk), tk, True)

    lam = (jnp.exp(jnp.sum(lam_ref[0:1, :] * lam_ref[1:2, :], axis=1, keepdims=True))
           - jnp.exp(jnp.sum(lam_ref[2:3, :] * lam_ref[3:4, :], axis=1, keepdims=True))
           + lambda_init)
    l0 = jnp.sum(l_scr[0], axis=1, keepdims=True)
    l1 = jnp.sum(l_scr[1], axis=1, keepdims=True)
    o = acc_scr[0] / l0 - lam * (acc_scr[1] / l1)
    ms = jnp.mean(o * o, axis=-1, keepdims=True)
    o_ref[...] = (o * lax.rsqrt(ms + LN_EPS) * g_ref[...] * (1.0 - lambda_init)).astype(BF16)


def _attn_call(qkv, lam_vecs, subln_g, lambda_init):
    assert ATT_TK == ATT_TQ
    tq = ATT_TQ
    n_q = SEQ // tq
    kern = functools.partial(_attn_kernel, lambda_init=lambda_init)
    blk = 2 * tq * V_DIM * 2 + 2 * SEQ * V_DIM * 2 + 4 * HEAD_DIM * 4 + V_DIM * 4
    scr = 2 * tq * V_DIM * 4 + 4 * tq * V7X_LANES * 4
    return pl.pallas_call(
        kern,
        out_shape=jax.ShapeDtypeStruct((N_TOK, D_MODEL), BF16),
        grid=(BATCH, N_HEADS, n_q),
        in_specs=[
            pl.BlockSpec((tq, V_DIM), lambda b, h, i: (b * n_q + i, h)),
            pl.BlockSpec((SEQ, V_DIM), lambda b, h, i: (b, N_HEADS + h)),
            pl.BlockSpec((SEQ, V_DIM), lambda b, h, i: (b, 2 * N_HEADS + h)),
            pl.BlockSpec((4, HEAD_DIM), lambda b, h, i: (0, 0)),
            pl.BlockSpec((1, V_DIM), lambda b, h, i: (0, 0)),
        ],
        out_specs=pl.BlockSpec((tq, V_DIM), lambda b, h, i: (b * n_q + i, h)),
        scratch_shapes=[pltpu.VMEM((2, tq, V7X_LANES), F32), pltpu.VMEM((2, tq, V7X_LANES), F32),
                        pltpu.VMEM((2, tq, V_DIM), F32)],
        compiler_params=_cparams(("parallel", "parallel", "arbitrary"),
                                 _vmem_limit(blk, scr, 8 * tq * ATT_TK * 4)),
        name="diff_attn",
    )(qkv, qkv, qkv, lam_vecs, subln_g)


def _wo_kernel(a_ref, w_ref, x_ref, mod_ref, lng_ref, lnb_ref, wr_ref,
               o_ref, h_ref, lg_ref):
    k = pl.program_id(1)
    tm = x_ref.shape[0]

    @pl.when(k == 0)
    def _():
        def prep(rows):
            o_ref[rows, :] = jnp.zeros((EPI_ROWS, o_ref.shape[1]), F32)
        _for_row_chunks(tm, prep)

    _accumulate_dot(o_ref, a_ref[...], w_ref)

    @pl.when(k == pl.num_programs(1) - 1)
    def _():
        def fin(rows):
            z = ALPHA * x_ref[rows, :] + (1.0 + mod_ref[2:3, :]) * o_ref[rows, :]
            xn = _layer_norm(z, lng_ref[...], lnb_ref[...])
            o_ref[rows, :] = xn
            h = xn * (1.0 + mod_ref[4:5, :]) + mod_ref[3:4, :]
            bits = lax.bitcast_convert_type(h.astype(BF16).astype(F32), jnp.uint32)
            half = bits.shape[1] // 2
            h_ref[rows, :] = bits[:, :half] | (bits[:, half:] >> 16)
            lane = lax.broadcasted_iota(jnp.int32, (EPI_ROWS, ROUTER_PAD), 1)
            lg = jnp.zeros((EPI_ROWS, ROUTER_PAD), F32)
            for e in range(N_EXPERTS):
                val = jnp.sum(h * wr_ref[e:e + 1, :], axis=1, keepdims=True)
                lg = jnp.where(lane == e, val, lg)
            lg_ref[rows, :] = lg
        _for_row_chunks(tm, fin)


def _wo_call(attn, w_o, x, mod, lng, lnb, w_router_t):
    tm, tk, d = WO_TM, WO_TK, D_MODEL
    tiles_per_seq = SEQ // tm
    blk = (tm * tk * 2 + tk * d * 2 + tm * d * 4 * 2 + tm * d * 2 + ADA_CHUNKS * d * 4
           + 2 * d * 4 + tm * ROUTER_PAD * 4)
    return pl.pallas_call(
        _wo_kernel,
        out_shape=(jax.ShapeDtypeStruct((N_TOK, d), F32),
                   jax.ShapeDtypeStruct((N_TOK, d // 2), jnp.uint32),
                   jax.ShapeDtypeStruct((N_TOK, ROUTER_PAD), F32)),
        grid=(N_TOK // tm, d // tk),
        in_specs=[
            pl.BlockSpec((tm, tk), lambda i, k: (i, k)),
            pl.BlockSpec((tk, d), lambda i, k: (k, 0)),
            pl.BlockSpec((tm, d), lambda i, k: (i, 0)),
            pl.BlockSpec((None, ADA_CHUNKS, d), lambda i, k: (i // tiles_per_seq, 0, 0)),
            pl.BlockSpec((1, d), lambda i, k: (0, 0)),
            pl.BlockSpec((1, d), lambda i, k: (0, 0)),
            pl.BlockSpec((N_EXPERTS, d), lambda i, k: (0, 0)),
        ],
        out_specs=(pl.BlockSpec((tm, d), lambda i, k: (i, 0)),
                   pl.BlockSpec((tm, d // 2), lambda i, k: (i, 0)),
                   pl.BlockSpec((tm, ROUTER_PAD), lambda i, k: (i, 0))),
        compiler_params=_cparams(("parallel", "arbitrary"),
                                 _vmem_limit(blk, 2 * N_EXPERTS * d * 4,
                                             tm * ACC_TN * 4 + 6 * EPI_ROWS * d * 4)),
        name="attn_out_proj",
    )(attn, w_o, x, mod, lng, lnb, w_router_t)


def _gather_row_copy(src_hbm, row, dst_buf, dst_row, sem):
    return pltpu.make_async_copy(src_hbm.at[pl.ds(row, 1), :], dst_buf.at[pl.ds(dst_row, 1), :], sem)


def _moe_kernel(be_ref, nu_ref, rt_ref, h_hbm, wg_ref, wu_ref, wd_ref, o_ref,
                hbuf, h_scr, sem):
    i = pl.program_id(0)
    k = pl.program_id(1)
    n_f = pl.num_programs(1)
    tm = h_scr.shape[0]
    half = h_scr.shape[1] // 2
    per_step = tm // MOE_N_F
    n_used = nu_ref[0]

    def start_rows(blk, first, count):
        def body(r, carry):
            row = first + r
            _gather_row_copy(h_hbm, rt_ref[blk * tm + row], hbuf, row, sem.at[0]).start()
            return carry
        lax.fori_loop(0, count, body, 0, unroll=DMA_UNROLL)

    def wait_rows():
        def body(r, carry):
            _gather_row_copy(h_hbm, 0, hbuf, r, sem.at[0]).wait()
            return carry
        lax.fori_loop(0, tm, body, 0, unroll=DMA_UNROLL)

    @pl.when((i >= n_used) & (k == 0))
    def _():
        def clear(rows):
            o_ref[rows, :] = jnp.zeros((EPI_ROWS, o_ref.shape[1]), F32)
        _for_row_chunks(tm, clear)

    @pl.when(i < n_used)
    def _():
        @pl.when((i == 0) & (k == 0))
        def _():
            start_rows(0, 0, tm)

        @pl.when(k == 0)
        def _():
            wait_rows()

            def prep(rows):
                w = hbuf[rows, :]
                hi = lax.bitcast_convert_type(w & jnp.uint32(0xFFFF0000), F32)
                lo = lax.bitcast_convert_type(w << 16, F32)
                h_scr[rows, 0:half] = hi.astype(BF16)
                h_scr[rows, half:] = lo.astype(BF16)
                o_ref[rows, :] = jnp.zeros((EPI_ROWS, o_ref.shape[1]), F32)
            _for_row_chunks(tm, prep)

        nxt = jnp.minimum(i + 1, n_used - 1)
        for r in range(per_step):
            row = k * per_step + r
            _gather_row_copy(h_hbm, rt_ref[nxt * tm + row], hbuf, row, sem.at[0]).start()

        h = h_scr[...]
        g = jnp.dot(h, wg_ref[...], preferred_element_type=F32)
        u = jnp.dot(h, wu_ref[...], preferred_element_type=F32)
        _accumulate_dot(o_ref, (_silu(g) * u).astype(BF16), wd_ref)

        @pl.when((i == n_used - 1) & (k == n_f - 1))
        def _():
            wait_rows()


def _moe_call(blk_expert, n_used, row_tok, h_packed, w_g, w_u, w_down):
    tm, tf, d = MOE_TM, MOE_TF, D_MODEL
    n_f = MOE_N_F

    def row_blk(i, nu):
        return jnp.minimum(i, nu[0] - 1)

    def f_blk(i, k, nu):
        return jnp.where(i < nu[0], k, n_f - 1)

    blk = 3 * d * tf * 2 + tm * d * 4
    scr = tm * (d // 2) * 4 + tm * d * 2
    grid_spec = pltpu.PrefetchScalarGridSpec(
        num_scalar_prefetch=3,
        grid=(MOE_N_BLK, n_f),
        in_specs=[
            pl.BlockSpec(memory_space=pl.ANY),
            pl.BlockSpec((None, d, tf),
                         lambda i, k, be, nu, rt: (be[row_blk(i, nu)], 0, f_blk(i, k, nu))),
            pl.BlockSpec((None, d, tf),
                         lambda i, k, be, nu, rt: (be[row_blk(i, nu)], 0, f_blk(i, k, nu))),
            pl.BlockSpec((None, tf, d),
                         lambda i, k, be, nu, rt: (be[row_blk(i, nu)], f_blk(i, k, nu), 0)),
        ],
        out_specs=pl.BlockSpec((tm, d), lambda i, k, be, nu, rt: (i, 0)),
        scratch_shapes=[pltpu.VMEM((tm, d // 2), jnp.uint32), pltpu.VMEM((tm, d), BF16),
                        pltpu.SemaphoreType.DMA((1,))],
    )
    return pl.pallas_call(
        _moe_kernel,
        out_shape=jax.ShapeDtypeStruct((MOE_N_ROWS, d), F32),
        grid_spec=grid_spec,
        compiler_params=pltpu.CompilerParams(
            dimension_semantics=("arbitrary", "arbitrary"),
            vmem_limit_bytes=_vmem_limit(blk, scr, 6 * tm * tf * 4 + tm * ACC_TN * 4),
            disable_bounds_checks=True),
        name="moe_ffn",
    )(blk_expert, n_used, row_tok, h_packed, w_g, w_u, w_down)


def _final_kernel(dest_ref, x_ref, y_hbm, tw_ref, mod_ref, lng_ref, lnb_ref, o_ref, ybuf, sem):
    i = pl.program_id(0)
    tm = x_ref.shape[0]
    slot = i % 2

    def start_tile(tile, dst_slot):
        def body(t, carry):
            for j in range(TOP_K):
                _gather_row_copy(y_hbm, dest_ref[(tile * tm + t) * TOP_K + j],
                                 ybuf.at[dst_slot, j], t, sem.at[dst_slot]).start()
            return carry
        lax.fori_loop(0, tm, body, 0, unroll=DMA_UNROLL)

    def wait_tile(dst_slot):
        def body(t, carry):
            for j in range(TOP_K):
                _gather_row_copy(y_hbm, 0, ybuf.at[dst_slot, j], t, sem.at[dst_slot]).wait()
            return carry
        lax.fori_loop(0, tm, body, 0, unroll=DMA_UNROLL)

    @pl.when(i == 0)
    def _():
        start_tile(0, 0)

    @pl.when(i + 1 < pl.num_programs(0))
    def _():
        start_tile(i + 1, 1 - slot)

    wait_tile(slot)

    def fin(rows):
        y = (tw_ref[rows, 0:1] * ybuf[slot, 0, rows, :]
             + tw_ref[rows, 1:2] * ybuf[slot, 1, rows, :])
        z = ALPHA * x_ref[rows, :] + (1.0 + mod_ref[5:6, :]) * y
        o_ref[rows, :] = _layer_norm(z, lng_ref[...], lnb_ref[...])
    _for_row_chunks(tm, fin)


def _final_call(dest, x, y_rows, top_w, mod, lng, lnb):
    tm, d = FIN_TM, D_MODEL
    tiles_per_seq = SEQ // tm
    blk = 2 * tm * d * 4 + tm * V7X_LANES * 4 + ADA_CHUNKS * d * 4 + 2 * d * 4
    grid_spec = pltpu.PrefetchScalarGridSpec(
        num_scalar_prefetch=1,
        grid=(N_TOK // tm,),
        in_specs=[
            pl.BlockSpec((tm, d), lambda i, de: (i, 0)),
            pl.BlockSpec(memory_space=pl.ANY),
            pl.BlockSpec((tm, TOP_K), lambda i, de: (i, 0)),
            pl.BlockSpec((None, ADA_CHUNKS, d), lambda i, de: (i // tiles_per_seq, 0, 0)),
            pl.BlockSpec((1, d), lambda i, de: (0, 0)),
            pl.BlockSpec((1, d), lambda i, de: (0, 0)),
        ],
        out_specs=pl.BlockSpec((tm, d), lambda i, de: (i, 0)),
        scratch_shapes=[pltpu.VMEM((2, TOP_K, tm, d), F32), pltpu.SemaphoreType.DMA((2,))],
    )
    return pl.pallas_call(
        _final_kernel,
        out_shape=jax.ShapeDtypeStruct((N_TOK, d), F32),
        grid_spec=grid_spec,
        compiler_params=pltpu.CompilerParams(
            dimension_semantics=("arbitrary",),
            vmem_limit_bytes=_vmem_limit(blk, 2 * TOP_K * tm * d * 4, 8 * EPI_ROWS * d * 4),
            disable_bounds_checks=True),
        name="final_ln",
    )(dest, x, y_rows, top_w, mod, lng, lnb)


def _cast_split_kernel(w_ref, g_ref, u_ref):
    n = g_ref.shape[1]

    def cast(rows):
        g_ref[rows, :] = w_ref[rows, 0:n].astype(BF16)
        u_ref[rows, :] = w_ref[rows, n:].astype(BF16)
    _for_row_chunks(w_ref.shape[0], cast)


def _cast_split_call(w):
    r, n2 = w.shape
    n = n2 // 2
    tm = CAST_TM
    blk = tm * n2 * 4 + 2 * tm * n * 2
    return pl.pallas_call(
        _cast_split_kernel,
        out_shape=(jax.ShapeDtypeStruct((r, n), BF16), jax.ShapeDtypeStruct((r, n), BF16)),
        grid=(r // tm,),
        in_specs=[pl.BlockSpec((tm, n2), lambda i: (i, 0))],
        out_specs=(pl.BlockSpec((tm, n), lambda i: (i, 0)),
                   pl.BlockSpec((tm, n), lambda i: (i, 0))),
        compiler_params=_cparams(("parallel",), _vmem_limit(blk, 0, 4 * EPI_ROWS * n2 * 4)),
        name="cast_gate_up",
    )(w)


def _route(logits):
    top_logit, top_idx = lax.top_k(logits, TOP_K)
    top_w = jax.nn.softmax(top_logit, axis=-1)
    e_flat = top_idx.reshape(MOE_N_ASG).astype(jnp.int32)
    tok_flat = jnp.arange(MOE_N_ASG, dtype=jnp.int32) // TOP_K
    onehot = (e_flat[:, None] == jnp.arange(N_EXPERTS, dtype=jnp.int32)[None, :]).astype(jnp.int32)
    csum = jnp.cumsum(onehot, axis=0)
    rank = jnp.sum((csum - onehot) * onehot, axis=1)
    counts = csum[-1]
    padded = (counts + MOE_TM - 1) // MOE_TM * MOE_TM
    padded_ends = jnp.cumsum(padded)
    padded_starts = padded_ends - padded
    dest = padded_starts[e_flat] + rank
    n_used = (padded_ends[-1:] // MOE_TM).astype(jnp.int32)
    blk_start = jnp.arange(MOE_N_BLK, dtype=jnp.int32) * MOE_TM
    blk_expert = jnp.minimum(jnp.searchsorted(padded_ends, blk_start, side='right'),
                             N_EXPERTS - 1).astype(jnp.int32)
    row_tok = jnp.zeros((MOE_N_ROWS,), jnp.int32).at[dest].set(
        tok_flat, unique_indices=True, mode='promise_in_bounds')
    return blk_expert, n_used, dest, row_tok, top_w


def _rope_tables(positions):
    inv_freq = ROPE_THETA ** (-jnp.arange(0, ROT_DIM, 2, dtype=F32) / ROT_DIM)
    ang = positions.astype(F32)[..., None] * inv_freq
    cos = jnp.cos(ang).reshape(N_TOK, ROT_HALF)
    sin = jnp.sin(ang).reshape(N_TOK, ROT_HALF)
    rest = HEAD_DIM - ROT_DIM
    c = jnp.concatenate([cos, cos, jnp.ones((N_TOK, rest), F32)], axis=1)
    s_up = jnp.concatenate([jnp.zeros((N_TOK, ROT_HALF), F32), sin,
                            jnp.zeros((N_TOK, rest), F32)], axis=1)
    s_dn = jnp.concatenate([-sin, jnp.zeros((N_TOK, HEAD_DIM - ROT_HALF), F32)], axis=1)
    return c, s_up, s_dn


def kernel(x, c, positions, ada_w, ada_b, ln_g, ln_b, kv_ada_w, kv_ada_b, w_pool, pool_scale,
           w_kv, w_q, w_o, lam_q1, lam_k1, lam_q2, lam_k2, subln_g, ffn_w_gu, ffn_w_down,
           router_w, moe_w_gu, moe_w_down):
    d = D_MODEL
    xt = x.reshape(N_TOK, d)

    c8 = jnp.pad(c, ((0, V7X_SUBLANES - BATCH), (0, 0)))
    mods = _ada_call(c8, ada_w, ada_b)[:, :BATCH].reshape(DEPTH, BATCH, ADA_CHUNKS, d)
    mod_kv = _ada_call(c8, kv_ada_w[None], kv_ada_b[None])[0, :BATCH].reshape(BATCH, 2, d)

    x1 = _pool_call(xt, mods[0], w_pool[0].astype(BF16), pool_scale[0][None],
                    ln_g[0, 0][None], ln_b[0, 0][None])
    x2 = _ffn_call(x1, mods[0], ffn_w_gu[0].astype(BF16), ffn_w_down[0].astype(BF16),
                   ln_g[0, 1][None], ln_b[0, 1][None])

    rope_c, rope_up, rope_dn = _rope_tables(positions)
    w_qkv = jnp.concatenate([w_q[0], w_kv], axis=1).astype(BF16)
    qkv = _qkv_call(x2, mods[1], mod_kv, w_qkv, rope_c, rope_up, rope_dn)
    lam_vecs = jnp.stack([lam_q1[0], lam_k1[0], lam_q2[0], lam_k2[0]], axis=0)
    lambda_init = 0.8 - 0.6 * math.exp(-0.3 * 1)
    attn = _attn_call(qkv, lam_vecs, subln_g[0][None], lambda_init)
    w_router_t = router_w[0].T
    x3, h3, logits = _wo_call(attn, w_o[0].astype(BF16), x2, mods[1],
                              ln_g[1, 0][None], ln_b[1, 0][None], w_router_t)

    blk_expert, n_used, dest, row_tok, top_w = _route(logits[:, :N_EXPERTS])
    w_g, w_u = _cast_split_call(moe_w_gu[0].reshape(N_EXPERTS * d, 2 * D_FF_EXPERT))
    y_rows = _moe_call(blk_expert, n_used, row_tok, h3,
                       w_g.reshape(N_EXPERTS, d, D_FF_EXPERT),
                       w_u.reshape(N_EXPERTS, d, D_FF_EXPERT),
                       moe_w_down[0].astype(BF16))
    out = _final_call(dest, x3, y_rows, top_w, mods[1], ln_g[1, 1][None], ln_b[1, 1][None])
    return out.reshape(BATCH, SEQ, d)
```

```python
import functools
import math

import jax
import jax.numpy as jnp
from jax import lax
from jax.experimental import pallas as pl
from jax.experimental.pallas import tpu as pltpu

F32 = jnp.float32
BF16 = jnp.bfloat16

D_MODEL = 4096
BATCH = 2
SEQ = 8192
N_TOK = BATCH * SEQ
DEPTH = 2
POOL_WINDOWS = (2, 4, 8, 16)
POOL_GROUP_DIM = D_MODEL // len(POOL_WINDOWS)
POOL_HALO = 16
N_HEADS = 16
HEAD_DIM = 128
V_DIM = 2 * HEAD_DIM
ROT_DIM = HEAD_DIM // 4
ROT_HALF = ROT_DIM // 2
ROPE_THETA = 500000.0
D_FF = 11008
N_EXPERTS = 8
TOP_K = 2
D_FF_EXPERT = D_MODEL
LN_EPS = 1e-5
ALPHA = (2.0 * DEPTH) ** 0.25
ADA_CHUNKS = 6
LOG2E = math.log2(math.e)
Q_PRESCALE = HEAD_DIM ** -0.5 * LOG2E

V7X_VMEM_BYTES = 64 * 1024 * 1024
V7X_LANES = 128
V7X_SUBLANES = 8

ADA_TN = 1024
ADA_TK = 512
POOL_TS = 256
FFN_TM = 512
FFN_TF = 256
QKV_TM = 512
QKV_TN = 1024
ATT_TQ = 512
ATT_TK = 512
ATT_UNROLL = 8
WO_TM = 512
WO_TK = 512
MOE_TM = 512
MOE_TF = 512
MOE_N_F = D_FF_EXPERT // MOE_TF
FIN_TM = 256
CAST_TM = 512
ACC_TN = 1024
EPI_ROWS = 64
DMA_UNROLL = 8
ROUTER_PAD = V7X_LANES

MOE_N_ASG = N_TOK * TOP_K
MOE_N_BLK = MOE_N_ASG // MOE_TM + N_EXPERTS
MOE_N_ROWS = MOE_N_BLK * MOE_TM


def _vmem_limit(pipelined_bytes, resident_bytes=0, temp_bytes=0):
    need = 2 * pipelined_bytes + resident_bytes + temp_bytes + (2 << 20)
    assert need <= V7X_VMEM_BYTES - (2 << 20), need
    return int(need)


def _cparams(sem, vmem):
    return pltpu.CompilerParams(dimension_semantics=sem, vmem_limit_bytes=vmem)


def _layer_norm(z, g, b):
    mu = jnp.mean(z, axis=-1, keepdims=True)
    zc = z - mu
    var = jnp.mean(zc * zc, axis=-1, keepdims=True)
    return zc * lax.rsqrt(var + LN_EPS) * g + b


def _silu(x):
    return x / (1.0 + jnp.exp(-x))


def _for_row_chunks(n_rows, fn):
    def body(r, carry):
        fn(pl.ds(pl.multiple_of(r * EPI_ROWS, EPI_ROWS), EPI_ROWS))
        return carry
    lax.fori_loop(0, n_rows // EPI_ROWS, body, 0)


def _accumulate_dot(o_ref, a, w_ref):
    for j in range(o_ref.shape[1] // ACC_TN):
        cols = slice(j * ACC_TN, (j + 1) * ACC_TN)
        o_ref[:, cols] += jnp.dot(a, w_ref[:, cols], preferred_element_type=F32)


def _ada_kernel(c_ref, w_ref, b_ref, o_ref):
    acc = jnp.zeros(o_ref.shape, F32) + b_ref[...]
    for kk in range(c_ref.shape[1] // ADA_TK):
        ks = slice(kk * ADA_TK, (kk + 1) * ADA_TK)
        cond = _silu(c_ref[:, ks]).astype(BF16)
        acc = acc + jnp.dot(cond, w_ref[ks, :].astype(BF16), preferred_element_type=F32)
    o_ref[...] = acc


def _ada_call(c8, w, b):
    n_l, d, n = w.shape
    tn = ADA_TN
    blk = d * tn * 4 + 8 * d * 4 + 8 * tn * 4 + tn * 4
    return pl.pallas_call(
        _ada_kernel,
        out_shape=jax.ShapeDtypeStruct((n_l, 8, n), F32),
        grid=(n_l, n // tn),
        in_specs=[
            pl.BlockSpec((8, d), lambda l, j: (0, 0)),
            pl.BlockSpec((None, d, tn), lambda l, j: (l, 0, j)),
            pl.BlockSpec((None, 1, tn), lambda l, j: (l, 0, j)),
        ],
        out_specs=pl.BlockSpec((None, 8, tn), lambda l, j: (l, 0, j)),
        compiler_params=_cparams(("parallel", "parallel"),
                                 _vmem_limit(blk, temp_bytes=4 * ADA_TK * tn * 4)),
        name="ada_mod",
    )(c8, w, b.reshape(n_l, 1, n))


def _pool_kernel(x_ref, halo_ref, mod_ref, wp_ref, ps_ref, lng_ref, lnb_ref, o_ref):
    i = pl.program_id(1)
    ts = x_ref.shape[0]
    t1 = (i * ts + 1 + lax.broadcasted_iota(jnp.int32, (ts, 1), 0)).astype(F32)
    for g, w in enumerate(POOL_WINDOWS):
        cols = slice(g * POOL_GROUP_DIM, (g + 1) * POOL_GROUP_DIM)
        sh = mod_ref[0:1, cols]
        sc = mod_ref[1:2, cols]
        gate = mod_ref[2:3, cols]
        x = x_ref[:, cols]
        h = x * (1.0 + sc) + sh
        hh = jnp.where(i > 0, halo_ref[:, cols] * (1.0 + sc) + sh, 0.0)
        s = jnp.concatenate([hh, h], axis=0)
        span = 1
        while span < w:
            s = s + pltpu.roll(s, span, 0)
            span *= 2
        win = s[POOL_HALO:, :]
        pooled = win / jnp.minimum(t1, float(w)) - h
        mixed = jnp.dot(pooled.astype(BF16), wp_ref[g], preferred_element_type=F32)
        o_ref[:, cols] = ALPHA * x + (1.0 + gate) * (mixed * ps_ref[:, cols])

    def ln_rows(rows):
        o_ref[rows, :] = _layer_norm(o_ref[rows, :], lng_ref[...], lnb_ref[...])
    _for_row_chunks(ts, ln_rows)


def _pool_call(x, mod, wp, ps, lng, lnb):
    ts = POOL_TS
    n_s = SEQ // ts
    halo_per_tile = ts // POOL_HALO
    d = D_MODEL
    blk = ts * d * 4 * 2 + POOL_HALO * d * 4 + ADA_CHUNKS * d * 4 + 3 * d * 4
    return pl.pallas_call(
        _pool_kernel,
        out_shape=jax.ShapeDtypeStruct((N_TOK, d), F32),
        grid=(BATCH, n_s),
        in_specs=[
            pl.BlockSpec((ts, d), lambda b, i: (b * n_s + i, 0)),
            pl.BlockSpec((POOL_HALO, d),
                         lambda b, i: (jnp.maximum((b * n_s + i) * halo_per_tile - 1, 0), 0)),
            pl.BlockSpec((None, ADA_CHUNKS, d), lambda b, i: (b, 0, 0)),
            pl.BlockSpec(wp.shape, lambda b, i: (0, 0, 0), pipeline_mode=pl.Buffered(1)),
            pl.BlockSpec((1, d), lambda b, i: (0, 0)),
            pl.BlockSpec((1, d), lambda b, i: (0, 0)),
            pl.BlockSpec((1, d), lambda b, i: (0, 0)),
        ],
        out_specs=pl.BlockSpec((ts, d), lambda b, i: (b * n_s + i, 0)),
        compiler_params=_cparams(
            ("parallel", "parallel"),
            _vmem_limit(blk, wp.size * 2, 8 * (ts + POOL_HALO) * POOL_GROUP_DIM * 4)),
        name="pool_mixer",
    )(x, x, mod, wp, ps, lng, lnb)


def _ffn_kernel(x_ref, mod_ref, wg_ref, wu_ref, wd_ref, lng_ref, lnb_ref, o_ref, h_scr):
    k = pl.program_id(1)
    tm = x_ref.shape[0]

    @pl.when(k == 0)
    def _():
        def prep(rows):
            h_scr[rows, :] = (x_ref[rows, :] * (1.0 + mod_ref[4:5, :])
                              + mod_ref[3:4, :]).astype(BF16)
            o_ref[rows, :] = jnp.zeros((EPI_ROWS, o_ref.shape[1]), F32)
        _for_row_chunks(tm, prep)

    h = h_scr[...]
    g = jnp.dot(h, wg_ref[...], preferred_element_type=F32)
    u = jnp.dot(h, wu_ref[...], preferred_element_type=F32)
    _accumulate_dot(o_ref, (_silu(g) * u).astype(BF16), wd_ref)

    @pl.when(k == pl.num_programs(1) - 1)
    def _():
        def fin(rows):
            z = ALPHA * x_ref[rows, :] + (1.0 + mod_ref[5:6, :]) * o_ref[rows, :]
            o_ref[rows, :] = _layer_norm(z, lng_ref[...], lnb_ref[...])
        _for_row_chunks(tm, fin)


def _ffn_call(x, mod, w_gu, w_down, lng, lnb):
    tm, tf, d = FFN_TM, FFN_TF, D_MODEL
    n_f = D_FF // tf
    tiles_per_seq = SEQ // tm
    blk = tm * d * 4 * 2 + ADA_CHUNKS * d * 4 + 3 * d * tf * 2 + 2 * d * 4
    return pl.pallas_call(
        _ffn_kernel,
        out_shape=jax.ShapeDtypeStruct((N_TOK, d), F32),
        grid=(N_TOK // tm, n_f),
        in_specs=[
            pl.BlockSpec((tm, d), lambda i, k: (i, 0)),
            pl.BlockSpec((None, ADA_CHUNKS, d), lambda i, k: (i // tiles_per_seq, 0, 0)),
            pl.BlockSpec((d, tf), lambda i, k: (0, k)),
            pl.BlockSpec((d, tf), lambda i, k: (0, k + n_f)),
            pl.BlockSpec((tf, d), lambda i, k: (k, 0)),
            pl.BlockSpec((1, d), lambda i, k: (0, 0)),
            pl.BlockSpec((1, d), lambda i, k: (0, 0)),
        ],
        out_specs=pl.BlockSpec((tm, d), lambda i, k: (i, 0)),
        scratch_shapes=[pltpu.VMEM((tm, d), BF16)],
        compiler_params=_cparams(("parallel", "arbitrary"),
                                 _vmem_limit(blk, tm * d * 2, 6 * tm * tf * 4 + tm * ACC_TN * 4)),
        name="dense_ffn",
    )(x, mod, w_gu, w_gu, w_down, lng, lnb)


def _rope(r, c, s_up, s_dn):
    outs = []
    for j in range(r.shape[1] // HEAD_DIM):
        xc = r[:, j * HEAD_DIM:(j + 1) * HEAD_DIM]
        outs.append(xc * c + pltpu.roll(xc, ROT_HALF, 1) * s_up
                    + pltpu.roll(xc, HEAD_DIM - ROT_HALF, 1) * s_dn)
    return jnp.concatenate(outs, axis=1)


def _qkv_kernel(x_ref, modm_ref, modkv_ref, w_ref, c_ref, su_ref, sd_ref, o_ref,
                hq_scr, hkv_scr, *, n_q, n_k):
    n = pl.program_id(1)

    @pl.when(n == 0)
    def _():
        def prep(rows):
            x = x_ref[rows, :]
            hq_scr[rows, :] = (x * (1.0 + modm_ref[1:2, :]) + modm_ref[0:1, :]).astype(BF16)
            hkv_scr[rows, :] = (x * (1.0 + modkv_ref[1:2, :]) + modkv_ref[0:1, :]).astype(BF16)
        _for_row_chunks(x_ref.shape[0], prep)

    @pl.when(n < n_q)
    def _():
        r = jnp.dot(hq_scr[...], w_ref[...], preferred_element_type=F32)
        r = _rope(r, c_ref[...], su_ref[...], sd_ref[...]) * Q_PRESCALE
        o_ref[...] = r.astype(BF16)

    @pl.when((n >= n_q) & (n < n_q + n_k))
    def _():
        r = jnp.dot(hkv_scr[...], w_ref[...], preferred_element_type=F32)
        o_ref[...] = _rope(r, c_ref[...], su_ref[...], sd_ref[...]).astype(BF16)

    @pl.when(n >= n_q + n_k)
    def _():
        o_ref[...] = jnp.dot(hkv_scr[...], w_ref[...],
                             preferred_element_type=F32).astype(BF16)


def _qkv_call(x, modm, modkv, w_qkv, rope_c, rope_up, rope_dn):
    tm, tn, d = QKV_TM, QKV_TN, D_MODEL
    n_out = w_qkv.shape[1]
    tiles_per_seq = SEQ // tm
    blk = tm * d * 4 + (ADA_CHUNKS + 2) * d * 4 + d * tn * 2 + 3 * tm * HEAD_DIM * 4 + tm * tn * 2
    kern = functools.partial(_qkv_kernel, n_q=D_MODEL // tn, n_k=D_MODEL // tn)
    return pl.pallas_call(
        kern,
        out_shape=jax.ShapeDtypeStruct((N_TOK, n_out), BF16),
        grid=(N_TOK // tm, n_out // tn),
        in_specs=[
            pl.BlockSpec((tm, d), lambda i, n: (i, 0)),
            pl.BlockSpec((None, ADA_CHUNKS, d), lambda i, n: (i // tiles_per_seq, 0, 0)),
            pl.BlockSpec((None, 2, d), lambda i, n: (i // tiles_per_seq, 0, 0)),
            pl.BlockSpec((d, tn), lambda i, n: (0, n)),
            pl.BlockSpec((tm, HEAD_DIM), lambda i, n: (i, 0)),
            pl.BlockSpec((tm, HEAD_DIM), lambda i, n: (i, 0)),
            pl.BlockSpec((tm, HEAD_DIM), lambda i, n: (i, 0)),
        ],
        out_specs=pl.BlockSpec((tm, tn), lambda i, n: (i, n)),
        scratch_shapes=[pltpu.VMEM((tm, d), BF16), pltpu.VMEM((tm, d), BF16)],
        compiler_params=_cparams(("parallel", "arbitrary"),
                                 _vmem_limit(blk, 2 * tm * d * 2, 4 * tm * tn * 4)),
        name="qkv_proj",
    )(x, modm, modkv, w_qkv, rope_c, rope_up, rope_dn)


def _attn_kernel(q_ref, k_ref, v_ref, lam_ref, g_ref, o_ref, m_scr, l_scr, acc_scr,
                 *, lambda_init):
    i = pl.program_id(2)
    tq = q_ref.shape[0]
    tk = ATT_TK
    m_scr[...] = jnp.full(m_scr.shape, -jnp.inf, F32)
    l_scr[...] = jnp.zeros(l_scr.shape, F32)
    acc_scr[...] = jnp.zeros(acc_scr.shape, F32)

    def chunk(start, width, masked):
        kc = k_ref[pl.ds(start, width), :]
        vc = v_ref[pl.ds(start, width), :]
        for c in range(2):
            qc = q_ref[:, c * HEAD_DIM:(c + 1) * HEAD_DIM]
            s = lax.dot_general(qc, kc[:, c * HEAD_DIM:(c + 1) * HEAD_DIM],
                                (((1,), (1,)), ((), ())), preferred_element_type=F32)
            if masked:
                row = lax.broadcasted_iota(jnp.int32, (tq, width), 0)
                col = lax.broadcasted_iota(jnp.int32, (tq, width), 1)
                s = jnp.where(col <= row, s, -jnp.inf)
            lanes = [s[:, t * V7X_LANES:(t + 1) * V7X_LANES] for t in range(width // V7X_LANES)]
            m_prev = m_scr[c]
            m_new = jnp.maximum(m_prev, jnp.max(functools.reduce(jnp.maximum, lanes),
                                                axis=1, keepdims=True))
            a = jnp.exp2(m_prev - m_new)
            ps = [jnp.exp2(x - m_new) for x in lanes]
            l_scr[c] = a * l_scr[c] + functools.reduce(jnp.add, ps)
            p = jnp.concatenate([x.astype(BF16) for x in ps], axis=1)
            acc_scr[c] = (jnp.concatenate([a, a], axis=1) * acc_scr[c]
                          + jnp.dot(p, vc, preferred_element_type=F32))
            m_scr[c] = m_new

    n_groups = i // ATT_UNROLL

    def body(j, carry):
        for u in range(ATT_UNROLL):
            chunk(pl.multiple_of((ATT_UNROLL * j + u) * tk, tk), tk, False)
        return carry

    lax.fori_loop(0, n_groups, body, 0)

    for r in range(ATT_UNROLL):
        @pl.when(i - n_groups * ATT_UNROLL == r)
        def _():
            for u in range(r):
                chunk(pl.multiple_of((n_groups * ATT_UNROLL + u) * tk, tk), tk, False)
            chunk(pl.multiple_of(i * tk, tk), tk, True)

    lam = (jnp.exp(jnp.sum(lam_ref[0:1, :] * lam_ref[1:2, :], axis=1, keepdims=True))
           - jnp.exp(jnp.sum(lam_ref[2:3, :] * lam_ref[3:4, :], axis=1, keepdims=True))
           + lambda_init)
    l0 = jnp.sum(l_scr[0], axis=1, keepdims=True)
    l1 = jnp.sum(l_scr[1], axis=1, keepdims=True)
    o = acc_scr[0] / l0 - lam * (acc_scr[1] / l1)
    ms = jnp.mean(o * o, axis=-1, keepdims=True)
    o_ref[...] = (o * lax.rsqrt(ms + LN_EPS) * g_ref[...] * (1.0 - lambda_init)).astype(BF16)


def _attn_call(qkv, lam_vecs, subln_g, lambda_init):
    assert ATT_TK == ATT_TQ
    tq = ATT_TQ
    n_q = SEQ // tq
    kern = functools.partial(_attn_kernel, lambda_init=lambda_init)
    blk = 2 * tq * V_DIM * 2 + 2 * SEQ * V_DIM * 2 + 4 * HEAD_DIM * 4 + V_DIM * 4
    scr = 2 * tq * V_DIM * 4 + 4 * tq * V7X_LANES * 4
    return pl.pallas_call(
        kern,
        out_shape=jax.ShapeDtypeStruct((N_TOK, D_MODEL), BF16),
        grid=(BATCH, N_HEADS, n_q),
        in_specs=[
            pl.BlockSpec((tq, V_DIM), lambda b, h, i: (b * n_q + i, h)),
            pl.BlockSpec((SEQ, V_DIM), lambda b, h, i: (b, N_HEADS + h)),
            pl.BlockSpec((SEQ, V_DIM), lambda b, h, i: (b, 2 * N_HEADS + h)),
            pl.BlockSpec((4, HEAD_DIM), lambda b, h, i: (0, 0)),
            pl.BlockSpec((1, V_DIM), lambda b, h, i: (0, 0)),
        ],
        out_specs=pl.BlockSpec((tq, V_DIM), lambda b, h, i: (b * n_q + i, h)),
        scratch_shapes=[pltpu.VMEM((2, tq, V7X_LANES), F32), pltpu.VMEM((2, tq, V7X_LANES), F32),
                        pltpu.VMEM((2, tq, V_DIM), F32)],
        compiler_params=_cparams(("parallel", "parallel", "arbitrary"),
                                 _vmem_limit(blk, scr, 8 * tq * ATT_TK * 4)),
        name="diff_attn",
    )(qkv, qkv, qkv, lam_vecs, subln_g)


def _wo_kernel(a_ref, w_ref, x_ref, mod_ref, lng_ref, lnb_ref, wr_ref,
               o_ref, h_ref, lg_ref):
    k = pl.program_id(1)
    tm = x_ref.shape[0]

    @pl.when(k == 0)
    def _():
        def prep(rows):
            o_ref[rows, :] = jnp.zeros((EPI_ROWS, o_ref.shape[1]), F32)
        _for_row_chunks(tm, prep)

    _accumulate_dot(o_ref, a_ref[...], w_ref)

    @pl.when(k == pl.num_programs(1) - 1)
    def _():
        def fin(rows):
            z = ALPHA * x_ref[rows, :] + (1.0 + mod_ref[2:3, :]) * o_ref[rows, :]
            xn = _layer_norm(z, lng_ref[...], lnb_ref[...])
            o_ref[rows, :] = xn
            h = xn * (1.0 + mod_ref[4:5, :]) + mod_ref[3:4, :]
            bits = lax.bitcast_convert_type(h.astype(BF16).astype(F32), jnp.uint32)
            half = bits.shape[1] // 2
            h_ref[rows, :] = bits[:, :half] | (bits[:, half:] >> 16)
            lane = lax.broadcasted_iota(jnp.int32, (EPI_ROWS, ROUTER_PAD), 1)
            lg = jnp.zeros((EPI_ROWS, ROUTER_PAD), F32)
            for e in range(N_EXPERTS):
                val = jnp.sum(h * wr_ref[e:e + 1, :], axis=1, keepdims=True)
                lg = jnp.where(lane == e, val, lg)
            lg_ref[rows, :] = lg
        _for_row_chunks(tm, fin)


def _wo_call(attn, w_o, x, mod, lng, lnb, w_router_t):
    tm, tk, d = WO_TM, WO_TK, D_MODEL
    tiles_per_seq = SEQ // tm
    blk = (tm * tk * 2 + tk * d * 2 + tm * d * 4 * 2 + tm * d * 2 + ADA_CHUNKS * d * 4
           + 2 * d * 4 + tm * ROUTER_PAD * 4)
    return pl.pallas_call(
        _wo_kernel,
        out_shape=(jax.ShapeDtypeStruct((N_TOK, d), F32),
                   jax.ShapeDtypeStruct((N_TOK, d // 2), jnp.uint32),
                   jax.ShapeDtypeStruct((N_TOK, ROUTER_PAD), F32)),
        grid=(N_TOK // tm, d // tk),
        in_specs=[
            pl.BlockSpec((tm, tk), lambda i, k: (i, k)),
            pl.BlockSpec((tk, d), lambda i, k: (k, 0)),
            pl.BlockSpec((tm, d), lambda i, k: (i, 0)),
            pl.BlockSpec((None, ADA_CHUNKS, d), lambda i, k: (i // tiles_per_seq, 0, 0)),
            pl.BlockSpec((1, d), lambda i, k: (0, 0)),
            pl.BlockSpec((1, d), lambda i, k: (0, 0)),
            pl.BlockSpec((N_EXPERTS, d), lambda i, k: (0, 0)),
        ],
        out_specs=(pl.BlockSpec((tm, d), lambda i, k: (i, 0)),
                   pl.BlockSpec((tm, d // 2), lambda i, k: (i, 0)),
                   pl.BlockSpec((tm, ROUTER_PAD), lambda i, k: (i, 0))),
        compiler_params=_cparams(("parallel", "arbitrary"),
                                 _vmem_limit(blk, 2 * N_EXPERTS * d * 4,
                                             tm * ACC_TN * 4 + 6 * EPI_ROWS * d * 4)),
        name="attn_out_proj",
    )(attn, w_o, x, mod, lng, lnb, w_router_t)


def _gather_row_copy(src_hbm, row, dst_buf, dst_row, sem):
    return pltpu.make_async_copy(src_hbm.at[pl.ds(row, 1), :], dst_buf.at[pl.ds(dst_row, 1), :], sem)


def _moe_kernel(be_ref, nu_ref, rt_ref, h_hbm, wg_ref, wu_ref, wd_ref, o_ref,
                hbuf, h_scr, sem):
    i = pl.program_id(0)
    k = pl.program_id(1)
    n_f = pl.num_programs(1)
    tm = h_scr.shape[0]
    half = h_scr.shape[1] // 2
    per_step = tm // MOE_N_F
    n_used = nu_ref[0]

    def start_rows(blk, first, count):
        def body(r, carry):
            row = first + r
            _gather_row_copy(h_hbm, rt_ref[blk * tm + row], hbuf, row, sem.at[0]).start()
            return carry
        lax.fori_loop(0, count, body, 0, unroll=DMA_UNROLL)

    def wait_rows():
        def body(r, carry):
            _gather_row_copy(h_hbm, 0, hbuf, r, sem.at[0]).wait()
            return carry
        lax.fori_loop(0, tm, body, 0, unroll=DMA_UNROLL)

    @pl.when((i >= n_used) & (k == 0))
    def _():
        def clear(rows):
            o_ref[rows, :] = jnp.zeros((EPI_ROWS, o_ref.shape[1]), F32)
        _for_row_chunks(tm, clear)

    @pl.when(i < n_used)
    def _():
        @pl.when((i == 0) & (k == 0))
        def _():
            start_rows(0, 0, tm)

        @pl.when(k == 0)
        def _():
            wait_rows()

            def prep(rows):
                w = hbuf[rows, :]
                hi = lax.bitcast_convert_type(w & jnp.uint32(0xFFFF0000), F32)
                lo = lax.bitcast_convert_type(w << 16, F32)
                h_scr[rows, 0:half] = hi.astype(BF16)
                h_scr[rows, half:] = lo.astype(BF16)
                o_ref[rows, :] = jnp.zeros((EPI_ROWS, o_ref.shape[1]), F32)
            _for_row_chunks(tm, prep)

        nxt = jnp.minimum(i + 1, n_used - 1)
        for r in range(per_step):
            row = k * per_step + r
            _gather_row_copy(h_hbm, rt_ref[nxt * tm + row], hbuf, row, sem.at[0]).start()

        h = h_scr[...]
        g = jnp.dot(h, wg_ref[...], preferred_element_type=F32)
        u = jnp.dot(h, wu_ref[...], preferred_element_type=F32)
        _accumulate_dot(o_ref, (_silu(g) * u).astype(BF16), wd_ref)

        @pl.when((i == n_used - 1) & (k == n_f - 1))
        def _():
            wait_rows()


def _moe_call(blk_expert, n_used, row_tok, h_packed, w_g, w_u, w_down):
    tm, tf, d = MOE_TM, MOE_TF, D_MODEL
    n_f = MOE_N_F

    def row_blk(i, nu):
        return jnp.minimum(i, nu[0] - 1)

    def f_blk(i, k, nu):
        return jnp.where(i < nu[0], k, n_f - 1)

    blk = 3 * d * tf * 2 + tm * d * 4
    scr = tm * (d // 2) * 4 + tm * d * 2
    grid_spec = pltpu.PrefetchScalarGridSpec(
        num_scalar_prefetch=3,
        grid=(MOE_N_BLK, n_f),
        in_specs=[
            pl.BlockSpec(memory_space=pl.ANY),
            pl.BlockSpec((None, d, tf),
                         lambda i, k, be, nu, rt: (be[row_blk(i, nu)], 0, f_blk(i, k, nu))),
            pl.BlockSpec((None, d, tf),
                         lambda i, k, be, nu, rt: (be[row_blk(i, nu)], 0, f_blk(i, k, nu))),
            pl.BlockSpec((None, tf, d),
                         lambda i, k, be, nu, rt: (be[row_blk(i, nu)], f_blk(i, k, nu), 0)),
        ],
        out_specs=pl.BlockSpec((tm, d), lambda i, k, be, nu, rt: (i, 0)),
        scratch_shapes=[pltpu.VMEM((tm, d // 2), jnp.uint32), pltpu.VMEM((tm, d), BF16),
                        pltpu.SemaphoreType.DMA((1,))],
    )
    return pl.pallas_call(
        _moe_kernel,
        out_shape=jax.ShapeDtypeStruct((MOE_N_ROWS, d), F32),
        grid_spec=grid_spec,
        compiler_params=pltpu.CompilerParams(
            dimension_semantics=("arbitrary", "arbitrary"),
            vmem_limit_bytes=_vmem_limit(blk, scr, 6 * tm * tf * 4 + tm * ACC_TN * 4),
            disable_bounds_checks=True),
        name="moe_ffn",
    )(blk_expert, n_used, row_tok, h_packed, w_g, w_u, w_down)


def _final_kernel(dest_ref, x_ref, y_hbm, tw_ref, mod_ref, lng_ref, lnb_ref, o_ref, ybuf, sem):
    i = pl.program_id(0)
    tm = x_ref.shape[0]
    slot = i % 2

    def start_tile(tile, dst_slot):
        def body(t, carry):
            for j in range(TOP_K):
                _gather_row_copy(y_hbm, dest_ref[(tile * tm + t) * TOP_K + j],
                                 ybuf.at[dst_slot, j], t, sem.at[dst_slot]).start()
            return carry
        lax.fori_loop(0, tm, body, 0, unroll=DMA_UNROLL)

    def wait_tile(dst_slot):
        def body(t, carry):
            for j in range(TOP_K):
                _gather_row_copy(y_hbm, 0, ybuf.at[dst_slot, j], t, sem.at[dst_slot]).wait()
            return carry
        lax.fori_loop(0, tm, body, 0, unroll=DMA_UNROLL)

    @pl.when(i == 0)
    def _():
        start_tile(0, 0)

    @pl.when(i + 1 < pl.num_programs(0))
    def _():
        start_tile(i + 1, 1 - slot)

    wait_tile(slot)

    def fin(rows):
        y = (tw_ref[rows, 0:1] * ybuf[slot, 0, rows, :]
             + tw_ref[rows, 1:2] * ybuf[slot, 1, rows, :])
        z = ALPHA * x_ref[rows, :] + (1.0 + mod_ref[5:6, :]) * y
        o_ref[rows, :] = _layer_norm(z, lng_ref[...], lnb_ref[...])
    _for_row_chunks(tm, fin)


def _final_call(dest, x, y_rows, top_w, mod, lng, lnb):
    tm, d = FIN_TM, D_MODEL
    tiles_per_seq = SEQ // tm
    blk = 2 * tm * d * 4 + tm * V7X_LANES * 4 + ADA_CHUNKS * d * 4 + 2 * d * 4
    grid_spec = pltpu.PrefetchScalarGridSpec(
        num_scalar_prefetch=1,
        grid=(N_TOK // tm,),
        in_specs=[
            pl.BlockSpec((tm, d), lambda i, de: (i, 0)),
            pl.BlockSpec(memory_space=pl.ANY),
            pl.BlockSpec((tm, TOP_K), lambda i, de: (i, 0)),
            pl.BlockSpec((None, ADA_CHUNKS, d), lambda i, de: (i // tiles_per_seq, 0, 0)),
            pl.BlockSpec((1, d), lambda i, de: (0, 0)),
            pl.BlockSpec((1, d), lambda i, de: (0, 0)),
        ],
        out_specs=pl.BlockSpec((tm, d), lambda i, de: (i, 0)),
        scratch_shapes=[pltpu.VMEM((2, TOP_K, tm, d), F32), pltpu.SemaphoreType.DMA((2,))],
    )
    return pl.pallas_call(
        _final_kernel,
        out_shape=jax.ShapeDtypeStruct((N_TOK, d), F32),
        grid_spec=grid_spec,
        compiler_params=pltpu.CompilerParams(
            dimension_semantics=("arbitrary",),
            vmem_limit_bytes=_vmem_limit(blk, 2 * TOP_K * tm * d * 4, 8 * EPI_ROWS * d * 4),
            disable_bounds_checks=True),
        name="final_ln",
    )(dest, x, y_rows, top_w, mod, lng, lnb)


def _cast_split_kernel(w_ref, g_ref, u_ref):
    n = g_ref.shape[1]

    def cast(rows):
        g_ref[rows, :] = w_ref[rows, 0:n].astype(BF16)
        u_ref[rows, :] = w_ref[rows, n:].astype(BF16)
    _for_row_chunks(w_ref.shape[0], cast)


def _cast_split_call(w):
    r, n2 = w.shape
    n = n2 // 2
    tm = CAST_TM
    blk = tm * n2 * 4 + 2 * tm * n * 2
    return pl.pallas_call(
        _cast_split_kernel,
        out_shape=(jax.ShapeDtypeStruct((r, n), BF16), jax.ShapeDtypeStruct((r, n), BF16)),
        grid=(r // tm,),
        in_specs=[pl.BlockSpec((tm, n2), lambda i: (i, 0))],
        out_specs=(pl.BlockSpec((tm, n), lambda i: (i, 0)),
                   pl.BlockSpec((tm, n), lambda i: (i, 0))),
        compiler_params=_cparams(("parallel",), _vmem_limit(blk, 0, 4 * EPI_ROWS * n2 * 4)),
        name="cast_gate_up",
    )(w)


def _route(logits):
    top_logit, top_idx = lax.top_k(logits, TOP_K)
    top_w = jax.nn.softmax(top_logit, axis=-1)
    e_flat = top_idx.reshape(MOE_N_ASG).astype(jnp.int32)
    tok_flat = jnp.arange(MOE_N_ASG, dtype=jnp.int32) // TOP_K
    onehot = (e_flat[:, None] == jnp.arange(N_EXPERTS, dtype=jnp.int32)[None, :]).astype(jnp.int32)
    csum = jnp.cumsum(onehot, axis=0)
    rank = jnp.sum((csum - onehot) * onehot, axis=1)
    counts = csum[-1]
    padded = (counts + MOE_TM - 1) // MOE_TM * MOE_TM
    padded_ends = jnp.cumsum(padded)
    padded_starts = padded_ends - padded
    dest = padded_starts[e_flat] + rank
    n_used = (padded_ends[-1:] // MOE_TM).astype(jnp.int32)
    blk_start = jnp.arange(MOE_N_BLK, dtype=jnp.int32) * MOE_TM
    blk_expert = jnp.minimum(jnp.searchsorted(padded_ends, blk_start, side='right'),
                             N_EXPERTS - 1).astype(jnp.int32)
    row_tok = jnp.zeros((MOE_N_ROWS,), jnp.int32).at[dest].set(
        tok_flat, unique_indices=True, mode='promise_in_bounds')
    return blk_expert, n_used, dest, row_tok, top_w


def _rope_tables(positions):
    inv_freq = ROPE_THETA ** (-jnp.arange(0, ROT_DIM, 2, dtype=F32) / ROT_DIM)
    ang = positions.astype(F32)[..., None] * inv_freq
    cos = jnp.cos(ang).reshape(N_TOK, ROT_HALF)
    sin = jnp.sin(ang).reshape(N_TOK, ROT_HALF)
    rest = HEAD_DIM - ROT_DIM
    c = jnp.concatenate([cos, cos, jnp.ones((N_TOK, rest), F32)], axis=1)
    s_up = jnp.concatenate([jnp.zeros((N_TOK, ROT_HALF), F32), sin,
                            jnp.zeros((N_TOK, rest), F32)], axis=1)
    s_dn = jnp.concatenate([-sin, jnp.zeros((N_TOK, HEAD_DIM - ROT_HALF), F32)], axis=1)
    return c, s_up, s_dn


def kernel(x, c, positions, ada_w, ada_b, ln_g, ln_b, kv_ada_w, kv_ada_b, w_pool, pool_scale,
           w_kv, w_q, w_o, lam_q1, lam_k1, lam_q2, lam_k2, subln_g, ffn_w_gu, ffn_w_down,
           router_w, moe_w_gu, moe_w_down):
    d = D_MODEL
    xt = x.reshape(N_TOK, d)

    c8 = jnp.pad(c, ((0, V7X_SUBLANES - BATCH), (0, 0)))
    mods = _ada_call(c8, ada_w, ada_b)[:, :BATCH].reshape(DEPTH, BATCH, ADA_CHUNKS, d)
    mod_kv = _ada_call(c8, kv_ada_w[None], kv_ada_b[None])[0, :BATCH].reshape(BATCH, 2, d)

    x1 = _pool_call(xt, mods[0], w_pool[0].astype(BF16), pool_scale[0][None],
                    ln_g[0, 0][None], ln_b[0, 0][None])
    x2 = _ffn_call(x1, mods[0], ffn_w_gu[0].astype(BF16), ffn_w_down[0].astype(BF16),
                   ln_g[0, 1][None], ln_b[0, 1][None])

    rope_c, rope_up, rope_dn = _rope_tables(positions)
    w_qkv = jnp.concatenate([w_q[0], w_kv], axis=1).astype(BF16)
    qkv = _qkv_call(x2, mods[1], mod_kv, w_qkv, rope_c, rope_up, rope_dn)
    lam_vecs = jnp.stack([lam_q1[0], lam_k1[0], lam_q2[0], lam_k2[0]], axis=0)
    lambda_init = 0.8 - 0.6 * math.exp(-0.3 * 1)
    attn = _attn_call(qkv, lam_vecs, subln_g[0][None], lambda_init)
    w_router_t = router_w[0].T
    x3, h3, logits = _wo_call(attn, w_o[0].astype(BF16), x2, mods[1],
                              ln_g[1, 0][None], ln_b[1, 0][None], w_router_t)

    blk_expert, n_used, dest, row_tok, top_w = _route(logits[:, :N_EXPERTS])
    w_g, w_u = _cast_split_call(moe_w_gu[0].reshape(N_EXPERTS * d, 2 * D_FF_EXPERT))
    y_rows = _moe_call(blk_expert, n_used, row_tok, h3,
                       w_g.reshape(N_EXPERTS, d, D_FF_EXPERT),
                       w_u.reshape(N_EXPERTS, d, D_FF_EXPERT),
                       moe_w_down[0].astype(BF16))
    out = _final_call(dest, x3, y_rows, top_w, mods[1], ln_g[1, 1][None], ln_b[1, 1][None])
    return out.reshape(BATCH, SEQ, d)
```

```python
import functools
import math

import jax
import jax.numpy as jnp
from jax import lax
from jax.experimental import pallas as pl
from jax.experimental.pallas import tpu as pltpu

F32 = jnp.float32
BF16 = jnp.bfloat16

D_MODEL = 4096
BATCH = 2
SEQ = 8192
N_TOK = BATCH * SEQ
DEPTH = 2
POOL_WINDOWS = (2, 4, 8, 16)
POOL_GROUP_DIM = D_MODEL // len(POOL_WINDOWS)
POOL_HALO = 16
N_HEADS = 16
HEAD_DIM = 128
V_DIM = 2 * HEAD_DIM
ROT_DIM = HEAD_DIM // 4
ROT_HALF = ROT_DIM // 2
ROPE_THETA = 500000.0
D_FF = 11008
N_EXPERTS = 8
TOP_K = 2
D_FF_EXPERT = D_MODEL
LN_EPS = 1e-5
ALPHA = (2.0 * DEPTH) ** 0.25
ADA_CHUNKS = 6
LOG2E = math.log2(math.e)
Q_PRESCALE = HEAD_DIM ** -0.5 * LOG2E

V7X_VMEM_BYTES = 64 * 1024 * 1024
V7X_LANES = 128
V7X_SUBLANES = 8

ADA_TN = 1024
ADA_TK = 512
POOL_TS = 256
FFN_TM = 512
FFN_TF = 256
QKV_TM = 512
QKV_TN = 1024
ATT_TQ = 512
ATT_TK = 512
ATT_UNROLL = 8
WO_TM = 512
WO_TK = 512
MOE_TM = 512
MOE_TF = 512
MOE_N_F = D_FF_EXPERT // MOE_TF
FIN_TM = 256
CAST_TM = 512
ACC_TN = 1024
EPI_ROWS = 64
DMA_UNROLL = 8
ROUTER_PAD = V7X_LANES

MOE_N_ASG = N_TOK * TOP_K
MOE_N_BLK = MOE_N_ASG // MOE_TM + N_EXPERTS
MOE_N_ROWS = MOE_N_BLK * MOE_TM


def _vmem_limit(pipelined_bytes, resident_bytes=0, temp_bytes=0):
    need = 2 * pipelined_bytes + resident_bytes + temp_bytes + (2 << 20)
    assert need <= V7X_VMEM_BYTES - (2 << 20), need
    return int(need)


def _cparams(sem, vmem):
    return pltpu.CompilerParams(dimension_semantics=sem, vmem_limit_bytes=vmem)


def _layer_norm(z, g, b):
    mu = jnp.mean(z, axis=-1, keepdims=True)
    zc = z - mu
    var = jnp.mean(zc * zc, axis=-1, keepdims=True)
    return zc * lax.rsqrt(var + LN_EPS) * g + b


def _silu(x):
    return x / (1.0 + jnp.exp(-x))


def _for_row_chunks(n_rows, fn):
    def body(r, carry):
        fn(pl.ds(pl.multiple_of(r * EPI_ROWS, EPI_ROWS), EPI_ROWS))
        return carry
    lax.fori_loop(0, n_rows // EPI_ROWS, body, 0)


def _accumulate_dot(o_ref, a, w_ref):
    for j in range(o_ref.shape[1] // ACC_TN):
        cols = slice(j * ACC_TN, (j + 1) * ACC_TN)
        o_ref[:, cols] += jnp.dot(a, w_ref[:, cols], preferred_element_type=F32)


def _ada_kernel(c_ref, w_ref, b_ref, o_ref):
    acc = jnp.zeros(o_ref.shape, F32) + b_ref[...]
    for kk in range(c_ref.shape[1] // ADA_TK):
        ks = slice(kk * ADA_TK, (kk + 1) * ADA_TK)
        cond = _silu(c_ref[:, ks]).astype(BF16)
        acc = acc + jnp.dot(cond, w_ref[ks, :].astype(BF16), preferred_element_type=F32)
    o_ref[...] = acc


def _ada_call(c8, w, b):
    n_l, d, n = w.shape
    tn = ADA_TN
    blk = d * tn * 4 + 8 * d * 4 + 8 * tn * 4 + tn * 4
    return pl.pallas_call(
        _ada_kernel,
        out_shape=jax.ShapeDtypeStruct((n_l, 8, n), F32),
        grid=(n_l, n // tn),
        in_specs=[
            pl.BlockSpec((8, d), lambda l, j: (0, 0)),
            pl.BlockSpec((None, d, tn), lambda l, j: (l, 0, j)),
            pl.BlockSpec((None, 1, tn), lambda l, j: (l, 0, j)),
        ],
        out_specs=pl.BlockSpec((None, 8, tn), lambda l, j: (l, 0, j)),
        compiler_params=_cparams(("parallel", "parallel"),
                                 _vmem_limit(blk, temp_bytes=4 * ADA_TK * tn * 4)),
        name="ada_mod",
    )(c8, w, b.reshape(n_l, 1, n))


def _pool_kernel(x_ref, halo_ref, mod_ref, wp_ref, ps_ref, lng_ref, lnb_ref, o_ref):
    i = pl.program_id(1)
    ts = x_ref.shape[0]
    t1 = (i * ts + 1 + lax.broadcasted_iota(jnp.int32, (ts, 1), 0)).astype(F32)
    for g, w in enumerate(POOL_WINDOWS):
        cols = slice(g * POOL_GROUP_DIM, (g + 1) * POOL_GROUP_DIM)
        sh = mod_ref[0:1, cols]
        sc = mod_ref[1:2, cols]
        gate = mod_ref[2:3, cols]
        x = x_ref[:, cols]
        h = x * (1.0 + sc) + sh
        hh = jnp.where(i > 0, halo_ref[:, cols] * (1.0 + sc) + sh, 0.0)
        s = jnp.concatenate([hh, h], axis=0)
        span = 1
        while span < w:
            s = s + pltpu.roll(s, span, 0)
            span *= 2
        win = s[POOL_HALO:, :]
        pooled = win / jnp.minimum(t1, float(w)) - h
        mixed = jnp.dot(pooled.astype(BF16), wp_ref[g], preferred_element_type=F32)
        o_ref[:, cols] = ALPHA * x + (1.0 + gate) * (mixed * ps_ref[:, cols])

    def ln_rows(rows):
        o_ref[rows, :] = _layer_norm(o_ref[rows, :], lng_ref[...], lnb_ref[...])
    _for_row_chunks(ts, ln_rows)


def _pool_call(x, mod, wp, ps, lng, lnb):
    ts = POOL_TS
    n_s = SEQ // ts
    halo_per_tile = ts // POOL_HALO
    d = D_MODEL
    blk = ts * d * 4 * 2 + POOL_HALO * d * 4 + ADA_CHUNKS * d * 4 + 3 * d * 4
    return pl.pallas_call(
        _pool_kernel,
        out_shape=jax.ShapeDtypeStruct((N_TOK, d), F32),
        grid=(BATCH, n_s),
        in_specs=[
            pl.BlockSpec((ts, d), lambda b, i: (b * n_s + i, 0)),
            pl.BlockSpec((POOL_HALO, d),
                         lambda b, i: (jnp.maximum((b * n_s + i) * halo_per_tile - 1, 0), 0)),
            pl.BlockSpec((None, ADA_CHUNKS, d), lambda b, i: (b, 0, 0)),
            pl.BlockSpec(wp.shape, lambda b, i: (0, 0, 0), pipeline_mode=pl.Buffered(1)),
            pl.BlockSpec((1, d), lambda b, i: (0, 0)),
            pl.BlockSpec((1, d), lambda b, i: (0, 0)),
            pl.BlockSpec((1, d), lambda b, i: (0, 0)),
        ],
        out_specs=pl.BlockSpec((ts, d), lambda b, i: (b * n_s + i, 0)),
        compiler_params=_cparams(
            ("parallel", "parallel"),
            _vmem_limit(blk, wp.size * 2, 8 * (ts + POOL_HALO) * POOL_GROUP_DIM * 4)),
        name="pool_mixer",
    )(x, x, mod, wp, ps, lng, lnb)


def _ffn_kernel(x_ref, mod_ref, wg_ref, wu_ref, wd_ref, lng_ref, lnb_ref, o_ref, h_scr):
    k = pl.program_id(1)
    tm = x_ref.shape[0]

    @pl.when(k == 0)
    def _():
        def prep(rows):
            h_scr[rows, :] = (x_ref[rows, :] * (1.0 + mod_ref[4:5, :])
                              + mod_ref[3:4, :]).astype(BF16)
            o_ref[rows, :] = jnp.zeros((EPI_ROWS, o_ref.shape[1]), F32)
        _for_row_chunks(tm, prep)

    h = h_scr[...]
    g = jnp.dot(h, wg_ref[...], preferred_element_type=F32)
    u = jnp.dot(h, wu_ref[...], preferred_element_type=F32)
    _accumulate_dot(o_ref, (_silu(g) * u).astype(BF16), wd_ref)

    @pl.when(k == pl.num_programs(1) - 1)
    def _():
        def fin(rows):
            z = ALPHA * x_ref[rows, :] + (1.0 + mod_ref[5:6, :]) * o_ref[rows, :]
            o_ref[rows, :] = _layer_norm(z, lng_ref[...], lnb_ref[...])
        _for_row_chunks(tm, fin)


def _ffn_call(x, mod, w_gu, w_down, lng, lnb):
    tm, tf, d = FFN_TM, FFN_TF, D_MODEL
    n_f = D_FF // tf
    tiles_per_seq = SEQ // tm
    blk = tm * d * 4 * 2 + ADA_CHUNKS * d * 4 + 3 * d * tf * 2 + 2 * d * 4
    return pl.pallas_call(
        _ffn_kernel,
        out_shape=jax.ShapeDtypeStruct((N_TOK, d), F32),
        grid=(N_TOK // tm, n_f),
        in_specs=[
            pl.BlockSpec((tm, d), lambda i, k: (i, 0)),
            pl.BlockSpec((None, ADA_CHUNKS, d), lambda i, k: (i // tiles_per_seq, 0, 0)),
            pl.BlockSpec((d, tf), lambda i, k: (0, k)),
            pl.BlockSpec((d, tf), lambda i, k: (0, k + n_f)),
            pl.BlockSpec((tf, d), lambda i, k: (k, 0)),
            pl.BlockSpec((1, d), lambda i, k: (0, 0)),
            pl.BlockSpec((1, d), lambda i, k: (0, 0)),
        ],
        out_specs=pl.BlockSpec((tm, d), lambda i, k: (i, 0)),
        scratch_shapes=[pltpu.VMEM((tm, d), BF16)],
        compiler_params=_cparams(("parallel", "arbitrary"),
                                 _vmem_limit(blk, tm * d * 2, 6 * tm * tf * 4 + tm * ACC_TN * 4)),
        name="dense_ffn",
    )(x, mod, w_gu, w_gu, w_down, lng, lnb)


def _rope(r, c, s_up, s_dn):
    outs = []
    for j in range(r.shape[1] // HEAD_DIM):
        xc = r[:, j * HEAD_DIM:(j + 1) * HEAD_DIM]
        outs.append(xc * c + pltpu.roll(xc, ROT_HALF, 1) * s_up
                    + pltpu.roll(xc, HEAD_DIM - ROT_HALF, 1) * s_dn)
    return jnp.concatenate(outs, axis=1)


def _qkv_kernel(x_ref, modm_ref, modkv_ref, w_ref, c_ref, su_ref, sd_ref, o_ref,
                hq_scr, hkv_scr, *, n_q, n_k):
    n = pl.program_id(1)

    @pl.when(n == 0)
    def _():
        def prep(rows):
            x = x_ref[rows, :]
            hq_scr[rows, :] = (x * (1.0 + modm_ref[1:2, :]) + modm_ref[0:1, :]).astype(BF16)
            hkv_scr[rows, :] = (x * (1.0 + modkv_ref[1:2, :]) + modkv_ref[0:1, :]).astype(BF16)
        _for_row_chunks(x_ref.shape[0], prep)

    @pl.when(n < n_q)
    def _():
        r = jnp.dot(hq_scr[...], w_ref[...], preferred_element_type=F32)
        r = _rope(r, c_ref[...], su_ref[...], sd_ref[...]) * Q_PRESCALE
        o_ref[...] = r.astype(BF16)

    @pl.when((n >= n_q) & (n < n_q + n_k))
    def _():
        r = jnp.dot(hkv_scr[...], w_ref[...], preferred_element_type=F32)
        o_ref[...] = _rope(r, c_ref[...], su_ref[...], sd_ref[...]).astype(BF16)

    @pl.when(n >= n_q + n_k)
    def _():
        o_ref[...] = jnp.dot(hkv_scr[...], w_ref[...],
                             preferred_element_type=F32).astype(BF16)


def _qkv_call(x, modm, modkv, w_qkv, rope_c, rope_up, rope_dn):
    tm, tn, d = QKV_TM, QKV_TN, D_MODEL
    n_out = w_qkv.shape[1]
    tiles_per_seq = SEQ // tm
    blk = tm * d * 4 + (ADA_CHUNKS + 2) * d * 4 + d * tn * 2 + 3 * tm * HEAD_DIM * 4 + tm * tn * 2
    kern = functools.partial(_qkv_kernel, n_q=D_MODEL // tn, n_k=D_MODEL // tn)
    return pl.pallas_call(
        kern,
        out_shape=jax.ShapeDtypeStruct((N_TOK, n_out), BF16),
        grid=(N_TOK // tm, n_out // tn),
        in_specs=[
            pl.BlockSpec((tm, d), lambda i, n: (i, 0)),
            pl.BlockSpec((None, ADA_CHUNKS, d), lambda i, n: (i // tiles_per_seq, 0, 0)),
            pl.BlockSpec((None, 2, d), lambda i, n: (i // tiles_per_seq, 0, 0)),
            pl.BlockSpec((d, tn), lambda i, n: (0, n)),
            pl.BlockSpec((tm, HEAD_DIM), lambda i, n: (i, 0)),
            pl.BlockSpec((tm, HEAD_DIM), lambda i, n: (i, 0)),
            pl.BlockSpec((tm, HEAD_DIM), lambda i, n: (i, 0)),
        ],
        out_specs=pl.BlockSpec((tm, tn), lambda i, n: (i, n)),
        scratch_shapes=[pltpu.VMEM((tm, d), BF16), pltpu.VMEM((tm, d), BF16)],
        compiler_params=_cparams(("parallel", "arbitrary"),
                                 _vmem_limit(blk, 2 * tm * d * 2, 4 * tm * tn * 4)),
        name="qkv_proj",
    )(x, modm, modkv, w_qkv, rope_c, rope_up, rope_dn)


def _attn_kernel(q_ref, k_ref, v_ref, lam_ref, g_ref, o_ref, m_scr, l_scr, acc_scr,
                 *, lambda_init):
    i = pl.program_id(2)
    tq = q_ref.shape[0]
    tk = ATT_TK
    m_scr[...] = jnp.full(m_scr.shape, -jnp.inf, F32)
    l_scr[...] = jnp.zeros(l_scr.shape, F32)
    acc_scr[...] = jnp.zeros(acc_scr.shape, F32)

    def chunk(start, width, masked):
        kc = k_ref[pl.ds(start, width), :]
        vc = v_ref[pl.ds(start, width), :]
        for c in range(2):
            qc = q_ref[:, c * HEAD_DIM:(c + 1) * HEAD_DIM]
            s = lax.dot_general(qc, kc[:, c * HEAD_DIM:(c + 1) * HEAD_DIM],
                                (((1,), (1,)), ((), ())), preferred_element_type=F32)
            if masked:
                row = lax.broadcasted_iota(jnp.int32, (tq, width), 0)
                col = lax.broadcasted_iota(jnp.int32, (tq, width), 1)
                s = jnp.where(col <= row, s, -jnp.inf)
            lanes = [s[:, t * V7X_LANES:(t + 1) * V7X_LANES] for t in range(width // V7X_LANES)]
            m_prev = m_scr[c]
            m_new = jnp.maximum(m_prev, jnp.max(functools.reduce(jnp.maximum, lanes),
                                                axis=1, keepdims=True))
            a = jnp.exp2(m_prev - m_new)
            ps = [jnp.exp2(x - m_new) for x in lanes]
            l_scr[c] = a * l_scr[c] + functools.reduce(jnp.add, ps)
            p = jnp.concatenate([x.astype(BF16) for x in ps], axis=1)
            acc_scr[c] = (jnp.concatenate([a, a], axis=1) * acc_scr[c]
                          + jnp.dot(p, vc, preferred_element_type=F32))
            m_scr[c] = m_new

    n_groups = i // ATT_UNROLL

    def body(j, carry):
        for u in range(ATT_UNROLL):
            chunk(pl.multiple_of((ATT_UNROLL * j + u) * tk, tk), tk, False)
        return carry

    lax.fori_loop(0, n_groups, body, 0)

    for r in range(ATT_UNROLL):
        @pl.when(i - n_groups * ATT_UNROLL == r)
        def _():
            for u in range(r):
                chunk(pl.multiple_of((n_groups * ATT_UNROLL + u) * tk, tk), tk, False)
            chunk(pl.multiple_of(i * tk, tk), tk, True)

    lam = (jnp.exp(jnp.sum(lam_ref[0:1, :] * lam_ref[1:2, :], axis=1, keepdims=True))
           - jnp.exp(jnp.sum(lam_ref[2:3, :] * lam_ref[3:4, :], axis=1, keepdims=True))
           + lambda_init)
    l0 = jnp.sum(l_scr[0], axis=1, keepdims=True)
    l1 = jnp.sum(l_scr[1], axis=1, keepdims=True)
    o = acc_scr[0] / l0 - lam * (acc_scr[1] / l1)
    ms = jnp.mean(o * o, axis=-1, keepdims=True)
    o_ref[...] = (o * lax.rsqrt(ms + LN_EPS) * g_ref[...] * (1.0 - lambda_init)).astype(BF16)


def _attn_call(qkv, lam_vecs, subln_g, lambda_init):
    assert ATT_TK == ATT_TQ
    tq = ATT_TQ
    n_q = SEQ // tq
    kern = functools.partial(_attn_kernel, lambda_init=lambda_init)
    blk = 2 * tq * V_DIM * 2 + 2 * SEQ * V_DIM * 2 + 4 * HEAD_DIM * 4 + V_DIM * 4
    scr = 2 * tq * V_DIM * 4 + 4 * tq * V7X_LANES * 4
    return pl.pallas_call(
        kern,
        out_shape=jax.ShapeDtypeStruct((N_TOK, D_MODEL), BF16),
        grid=(BATCH, N_HEADS, n_q),
        in_specs=[
            pl.BlockSpec((tq, V_DIM), lambda b, h, i: (b * n_q + i, h)),
            pl.BlockSpec((SEQ, V_DIM), lambda b, h, i: (b, N_HEADS + h)),
            pl.BlockSpec((SEQ, V_DIM), lambda b, h, i: (b, 2 * N_HEADS + h)),
            pl.BlockSpec((4, HEAD_DIM), lambda b, h, i: (0, 0)),
            pl.BlockSpec((1, V_DIM), lambda b, h, i: (0, 0)),
        ],
        out_specs=pl.BlockSpec((tq, V_DIM), lambda b, h, i: (b * n_q + i, h)),
        scratch_shapes=[pltpu.VMEM((2, tq, V7X_LANES), F32), pltpu.VMEM((2, tq, V7X_LANES), F32),
                        pltpu.VMEM((2, tq, V_DIM), F32)],
        compiler_params=_cparams(("parallel", "parallel", "arbitrary"),
                                 _vmem_limit(blk, scr, 8 * tq * ATT_TK * 4)),
        name="diff_attn",
    )(qkv, qkv, qkv, lam_vecs, subln_g)


def _wo_kernel(a_ref, w_ref, x_ref, mod_ref, lng_ref, lnb_ref, wr_ref,
               o_ref, h_ref, lg_ref):
    k = pl.program_id(1)
    tm = x_ref.shape[0]

    @pl.when(k == 0)
    def _():
        def prep(rows):
            o_ref[rows, :] = jnp.zeros((EPI_ROWS, o_ref.shape[1]), F32)
        _for_row_chunks(tm, prep)

    _accumulate_dot(o_ref, a_ref[...], w_ref)

    @pl.when(k == pl.num_programs(1) - 1)
    def _():
        def fin(rows):
            z = ALPHA * x_ref[rows, :] + (1.0 + mod_ref[2:3, :]) * o_ref[rows, :]
            xn = _layer_norm(z, lng_ref[...], lnb_ref[...])
            o_ref[rows, :] = xn
            h = xn * (1.0 + mod_ref[4:5, :]) + mod_ref[3:4, :]
            bits = lax.bitcast_convert_type(h.astype(BF16).astype(F32), jnp.uint32)
            half = bits.shape[1] // 2
            h_ref[rows, :] = bits[:, :half] | (bits[:, half:] >> 16)
            lane = lax.broadcasted_iota(jnp.int32, (EPI_ROWS, ROUTER_PAD), 1)
            lg = jnp.zeros((EPI_ROWS, ROUTER_PAD), F32)
            for e in range(N_EXPERTS):
                val = jnp.sum(h * wr_ref[e:e + 1, :], axis=1, keepdims=True)
                lg = jnp.where(lane == e, val, lg)
            lg_ref[rows, :] = lg
        _for_row_chunks(tm, fin)


def _wo_call(attn, w_o, x, mod, lng, lnb, w_router_t):
    tm, tk, d = WO_TM, WO_TK, D_MODEL
    tiles_per_seq = SEQ // tm
    blk = (tm * tk * 2 + tk * d * 2 + tm * d * 4 * 2 + tm * d * 2 + ADA_CHUNKS * d * 4
           + 2 * d * 4 + tm * ROUTER_PAD * 4)
    return pl.pallas_call(
        _wo_kernel,
        out_shape=(jax.ShapeDtypeStruct((N_TOK, d), F32),
                   jax.ShapeDtypeStruct((N_TOK, d // 2), jnp.uint32),
                   jax.ShapeDtypeStruct((N_TOK, ROUTER_PAD), F32)),
        grid=(N_TOK // tm, d // tk),
        in_specs=[
            pl.BlockSpec((tm, tk), lambda i, k: (i, k)),
            pl.BlockSpec((tk, d), lambda i, k: (k, 0)),
            pl.BlockSpec((tm, d), lambda i, k: (i, 0)),
            pl.BlockSpec((None, ADA_CHUNKS, d), lambda i, k: (i // tiles_per_seq, 0, 0)),
            pl.BlockSpec((1, d), lambda i, k: (0, 0)),
            pl.BlockSpec((1, d), lambda i, k: (0, 0)),
            pl.BlockSpec((N_EXPERTS, d), lambda i, k: (0, 0)),
        ],
        out_specs=(pl.BlockSpec((tm, d), lambda i, k: (i, 0)),
                   pl.BlockSpec((tm, d // 2), lambda i, k: (i, 0)),
                   pl.BlockSpec((tm, ROUTER_PAD), lambda i, k: (i, 0))),
        compiler_params=_cparams(("parallel", "arbitrary"),
                                 _vmem_limit(blk, 2 * N_EXPERTS * d * 4,
                                             tm * ACC_TN * 4 + 6 * EPI_ROWS * d * 4)),
        name="attn_out_proj",
    )(attn, w_o, x, mod, lng, lnb, w_router_t)


def _gather_row_copy(src_hbm, row, dst_buf, dst_row, sem):
    return pltpu.make_async_copy(src_hbm.at[pl.ds(row, 1), :], dst_buf.at[pl.ds(dst_row, 1), :], sem)


def _moe_kernel(be_ref, nu_ref, rt_ref, h_hbm, wg_ref, wu_ref, wd_ref, o_ref,
                hbuf, h_scr, sem):
    i = pl.program_id(0)
    k = pl.program_id(1)
    n_f = pl.num_programs(1)
    tm = h_scr.shape[0]
    half = h_scr.shape[1] // 2
    per_step = tm // MOE_N_F
    n_used = nu_ref[0]

    def start_rows(blk, first, count):
        def body(r, carry):
            row = first + r
            _gather_row_copy(h_hbm, rt_ref[blk * tm + row], hbuf, row, sem.at[0]).start()
            return carry
        lax.fori_loop(0, count, body, 0, unroll=DMA_UNROLL)

    def wait_rows():
        def body(r, carry):
            _gather_row_copy(h_hbm, 0, hbuf, r, sem.at[0]).wait()
            return carry
        lax.fori_loop(0, tm, body, 0, unroll=DMA_UNROLL)

    @pl.when((i >= n_used) & (k == 0))
    def _():
        def clear(rows):
            o_ref[rows, :] = jnp.zeros((EPI_ROWS, o_ref.shape[1]), F32)
        _for_row_chunks(tm, clear)

    @pl.when(i < n_used)
    def _():
        @pl.when((i == 0) & (k == 0))
        def _():
            start_rows(0, 0, tm)

        @pl.when(k == 0)
        def _():
            wait_rows()

            def prep(rows):
                w = hbuf[rows, :]
                hi = lax.bitcast_convert_type(w & jnp.uint32(0xFFFF0000), F32)
                lo = lax.bitcast_convert_type(w << 16, F32)
                h_scr[rows, 0:half] = hi.astype(BF16)
                h_scr[rows, half:] = lo.astype(BF16)
                o_ref[rows, :] = jnp.zeros((EPI_ROWS, o_ref.shape[1]), F32)
            _for_row_chunks(tm, prep)

        nxt = jnp.minimum(i + 1, n_used - 1)
        for r in range(per_step):
            row = k * per_step + r
            _gather_row_copy(h_hbm, rt_ref[nxt * tm + row], hbuf, row, sem.at[0]).start()

        h = h_scr[...]
        g = jnp.dot(h, wg_ref[...], preferred_element_type=F32)
        u = jnp.dot(h, wu_ref[...], preferred_element_type=F32)
        _accumulate_dot(o_ref, (_silu(g) * u).astype(BF16), wd_ref)

        @pl.when((i == n_used - 1) & (k == n_f - 1))
        def _():
            wait_rows()


def _moe_call(blk_expert, n_used, row_tok, h_packed, w_g, w_u, w_down):
    tm, tf, d = MOE_TM, MOE_TF, D_MODEL
    n_f = MOE_N_F

    def row_blk(i, nu):
        return jnp.minimum(i, nu[0] - 1)

    def f_blk(i, k, nu):
        return jnp.where(i < nu[0], k, n_f - 1)

    blk = 3 * d * tf * 2 + tm * d * 4
    scr = tm * (d // 2) * 4 + tm * d * 2
    grid_spec = pltpu.PrefetchScalarGridSpec(
        num_scalar_prefetch=3,
        grid=(MOE_N_BLK, n_f),
        in_specs=[
            pl.BlockSpec(memory_space=pl.ANY),
            pl.BlockSpec((None, d, tf),
                         lambda i, k, be, nu, rt: (be[row_blk(i, nu)], 0, f_blk(i, k, nu))),
            pl.BlockSpec((None, d, tf),
                         lambda i, k, be, nu, rt: (be[row_blk(i, nu)], 0, f_blk(i, k, nu))),
            pl.BlockSpec((None, tf, d),
                         lambda i, k, be, nu, rt: (be[row_blk(i, nu)], f_blk(i, k, nu), 0)),
        ],
        out_specs=pl.BlockSpec((tm, d), lambda i, k, be, nu, rt: (i, 0)),
        scratch_shapes=[pltpu.VMEM((tm, d // 2), jnp.uint32), pltpu.VMEM((tm, d), BF16),
                        pltpu.SemaphoreType.DMA((1,))],
    )
    return pl.pallas_call(
        _moe_kernel,
        out_shape=jax.ShapeDtypeStruct((MOE_N_ROWS, d), F32),
        grid_spec=grid_spec,
        compiler_params=pltpu.CompilerParams(
            dimension_semantics=("arbitrary", "arbitrary"),
            vmem_limit_bytes=_vmem_limit(blk, scr, 6 * tm * tf * 4 + tm * ACC_TN * 4),
            disable_bounds_checks=True),
        name="moe_ffn",
    )(blk_expert, n_used, row_tok, h_packed, w_g, w_u, w_down)


def _final_kernel(dest_ref, x_ref, y_hbm, tw_ref, mod_ref, lng_ref, lnb_ref, o_ref, ybuf, sem):
    i = pl.program_id(0)
    tm = x_ref.shape[0]
    slot = i % 2

    def start_tile(tile, dst_slot):
        def body(t, carry):
            for j in range(TOP_K):
                _gather_row_copy(y_hbm, dest_ref[(tile * tm + t) * TOP_K + j],
                                 ybuf.at[dst_slot, j], t, sem.at[dst_slot]).start()
            return carry
        lax.fori_loop(0, tm, body, 0, unroll=DMA_UNROLL)

    def wait_tile(dst_slot):
        def body(t, carry):
            for j in range(TOP_K):
                _gather_row_copy(y_hbm, 0, ybuf.at[dst_slot, j], t, sem.at[dst_slot]).wait()
            return carry
        lax.fori_loop(0, tm, body, 0, unroll=DMA_UNROLL)

    @pl.when(i == 0)
    def _():
        start_tile(0, 0)

    wait_tile(slot)

    n_tiles = pl.num_programs(0)
    nxt = jnp.minimum(i + 1, n_tiles - 1)

    def body(r, carry):
        base = pl.multiple_of(r * EPI_ROWS, EPI_ROWS)
        rows = pl.ds(base, EPI_ROWS)
        for t in range(EPI_ROWS):
            for j in range(TOP_K):
                _gather_row_copy(y_hbm, dest_ref[(nxt * tm + base + t) * TOP_K + j],
                                 ybuf.at[1 - slot, j], base + t,
                                 sem.at[1 - slot]).start(priority=j)
        y = (tw_ref[rows, 0:1] * ybuf[slot, 0, rows, :]
             + tw_ref[rows, 1:2] * ybuf[slot, 1, rows, :])
        z = ALPHA * x_ref[rows, :] + (1.0 + mod_ref[5:6, :]) * y
        o_ref[rows, :] = _layer_norm(z, lng_ref[...], lnb_ref[...])
        return carry

    lax.fori_loop(0, tm // EPI_ROWS, body, 0)

    @pl.when(i == n_tiles - 1)
    def _():
        wait_tile(1 - slot)


def _final_call(dest, x, y_rows, top_w, mod, lng, lnb):
    tm, d = FIN_TM, D_MODEL
    tiles_per_seq = SEQ // tm
    blk = 2 * tm * d * 4 + tm * V7X_LANES * 4 + ADA_CHUNKS * d * 4 + 2 * d * 4
    grid_spec = pltpu.PrefetchScalarGridSpec(
        num_scalar_prefetch=1,
        grid=(N_TOK // tm,),
        in_specs=[
            pl.BlockSpec((tm, d), lambda i, de: (i, 0)),
            pl.BlockSpec(memory_space=pl.ANY),
            pl.BlockSpec((tm, TOP_K), lambda i, de: (i, 0)),
            pl.BlockSpec((None, ADA_CHUNKS, d), lambda i, de: (i // tiles_per_seq, 0, 0)),
            pl.BlockSpec((1, d), lambda i, de: (0, 0)),
            pl.BlockSpec((1, d), lambda i, de: (0, 0)),
        ],
        out_specs=pl.BlockSpec((tm, d), lambda i, de: (i, 0)),
        scratch_shapes=[pltpu.VMEM((2, TOP_K, tm, d), F32), pltpu.SemaphoreType.DMA((2,))],
    )
    return pl.pallas_call(
        _final_kernel,
        out_shape=jax.ShapeDtypeStruct((N_TOK, d), F32),
        grid_spec=grid_spec,
        compiler_params=pltpu.CompilerParams(
            dimension_semantics=("arbitrary",),
            vmem_limit_bytes=_vmem_limit(blk, 2 * TOP_K * tm * d * 4, 8 * EPI_ROWS * d * 4),
            disable_bounds_checks=True),
        name="final_ln",
    )(dest, x, y_rows, top_w, mod, lng, lnb)


def _cast_split_kernel(w_ref, g_ref, u_ref):
    n = g_ref.shape[1]

    def cast(rows):
        g_ref[rows, :] = w_ref[rows, 0:n].astype(BF16)
        u_ref[rows, :] = w_ref[rows, n:].astype(BF16)
    _for_row_chunks(w_ref.shape[0], cast)


def _cast_split_call(w):
    r, n2 = w.shape
    n = n2 // 2
    tm = CAST_TM
    blk = tm * n2 * 4 + 2 * tm * n * 2
    return pl.pallas_call(
        _cast_split_kernel,
        out_shape=(jax.ShapeDtypeStruct((r, n), BF16), jax.ShapeDtypeStruct((r, n), BF16)),
        grid=(r // tm,),
        in_specs=[pl.BlockSpec((tm, n2), lambda i: (i, 0))],
        out_specs=(pl.BlockSpec((tm, n), lambda i: (i, 0)),
                   pl.BlockSpec((tm, n), lambda i: (i, 0))),
        compiler_params=_cparams(("parallel",), _vmem_limit(blk, 0, 4 * EPI_ROWS * n2 * 4)),
        name="cast_gate_up",
    )(w)


def _route(logits):
    top_logit, top_idx = lax.top_k(logits, TOP_K)
    top_w = jax.nn.softmax(top_logit, axis=-1)
    e_flat = top_idx.reshape(MOE_N_ASG).astype(jnp.int32)
    tok_flat = jnp.arange(MOE_N_ASG, dtype=jnp.int32) // TOP_K
    onehot = (e_flat[:, None] == jnp.arange(N_EXPERTS, dtype=jnp.int32)[None, :]).astype(jnp.int32)
    csum = jnp.cumsum(onehot, axis=0)
    rank = jnp.sum((csum - onehot) * onehot, axis=1)
    counts = csum[-1]
    padded = (counts + MOE_TM - 1) // MOE_TM * MOE_TM
    padded_ends = jnp.cumsum(padded)
    padded_starts = padded_ends - padded
    dest = padded_starts[e_flat] + rank
    n_used = (padded_ends[-1:] // MOE_TM).astype(jnp.int32)
    blk_start = jnp.arange(MOE_N_BLK, dtype=jnp.int32) * MOE_TM
    blk_expert = jnp.minimum(jnp.searchsorted(padded_ends, blk_start, side='right'),
                             N_EXPERTS - 1).astype(jnp.int32)
    row_tok = jnp.zeros((MOE_N_ROWS,), jnp.int32).at[dest].set(
        tok_flat, unique_indices=True, mode='promise_in_bounds')
    return blk_expert, n_used, dest, row_tok, top_w


def _rope_tables(positions):
    inv_freq = ROPE_THETA ** (-jnp.arange(0, ROT_DIM, 2, dtype=F32) / ROT_DIM)
    ang = positions.astype(F32)[..., None] * inv_freq
    cos = jnp.cos(ang).reshape(N_TOK, ROT_HALF)
    sin = jnp.sin(ang).reshape(N_TOK, ROT_HALF)
    rest = HEAD_DIM - ROT_DIM
    c = jnp.concatenate([cos, cos, jnp.ones((N_TOK, rest), F32)], axis=1)
    s_up = jnp.concatenate([jnp.zeros((N_TOK, ROT_HALF), F32), sin,
                            jnp.zeros((N_TOK, rest), F32)], axis=1)
    s_dn = jnp.concatenate([-sin, jnp.zeros((N_TOK, HEAD_DIM - ROT_HALF), F32)], axis=1)
    return c, s_up, s_dn


def kernel(x, c, positions, ada_w, ada_b, ln_g, ln_b, kv_ada_w, kv_ada_b, w_pool, pool_scale,
           w_kv, w_q, w_o, lam_q1, lam_k1, lam_q2, lam_k2, subln_g, ffn_w_gu, ffn_w_down,
           router_w, moe_w_gu, moe_w_down):
    d = D_MODEL
    xt = x.reshape(N_TOK, d)

    c8 = jnp.pad(c, ((0, V7X_SUBLANES - BATCH), (0, 0)))
    mods = _ada_call(c8, ada_w, ada_b)[:, :BATCH].reshape(DEPTH, BATCH, ADA_CHUNKS, d)
    mod_kv = _ada_call(c8, kv_ada_w[None], kv_ada_b[None])[0, :BATCH].reshape(BATCH, 2, d)

    x1 = _pool_call(xt, mods[0], w_pool[0].astype(BF16), pool_scale[0][None],
                    ln_g[0, 0][None], ln_b[0, 0][None])
    x2 = _ffn_call(x1, mods[0], ffn_w_gu[0].astype(BF16), ffn_w_down[0].astype(BF16),
                   ln_g[0, 1][None], ln_b[0, 1][None])

    rope_c, rope_up, rope_dn = _rope_tables(positions)
    w_qkv = jnp.concatenate([w_q[0], w_kv], axis=1).astype(BF16)
    qkv = _qkv_call(x2, mods[1], mod_kv, w_qkv, rope_c, rope_up, rope_dn)
    lam_vecs = jnp.stack([lam_q1[0], lam_k1[0], lam_q2[0], lam_k2[0]], axis=0)
    lambda_init = 0.8 - 0.6 * math.exp(-0.3 * 1)
    attn = _attn_call(qkv, lam_vecs, subln_g[0][None], lambda_init)
    w_router_t = router_w[0].T
    x3, h3, logits = _wo_call(attn, w_o[0].astype(BF16), x2, mods[1],
                              ln_g[1, 0][None], ln_b[1, 0][None], w_router_t)

    blk_expert, n_used, dest, row_tok, top_w = _route(logits[:, :N_EXPERTS])
    w_g, w_u = _cast_split_call(moe_w_gu[0].reshape(N_EXPERTS * d, 2 * D_FF_EXPERT))
    y_rows = _moe_call(blk_expert, n_used, row_tok, h3,
                       w_g.reshape(N_EXPERTS, d, D_FF_EXPERT),
                       w_u.reshape(N_EXPERTS, d, D_FF_EXPERT),
                       moe_w_down[0].astype(BF16))
    out = _final_call(dest, x3, y_rows, top_w, mods[1], ln_g[1, 1][None], ln_b[1, 1][None])
    return out.reshape(BATCH, SEQ, d)
```
